```python
import math
import jax, jax.numpy as jnp
from jax import lax
import numpy as np

D_MODEL = 1024
BATCH = 16
SEQ = 2048
DEPTH = 2

CTX_LEN = 256
GRID_W = 64
HEAD_DIM = 64
GROUP_WIDTH = D_MODEL // 4
N_MOD = 6
EPS = 1e-6

ATT_HEADS = GROUP_WIDTH // HEAD_DIM
ATT_KV_HEADS = ATT_HEADS // 2
ATT_GROUP = ATT_HEADS // ATT_KV_HEADS
Q_BLOCK = 128
ROPE_BASE = 10000.0

S5_CH = GROUP_WIDTH
S5_GROUP_CH = 16
S5_GROUPS = S5_CH // S5_GROUP_CH
S5_STATE = 64
S5_DT_MIN = 1e-3
S5_DT_MAX = 1e-1
S5_MIN_DECAY = 1e-4

NA_HEADS = GROUP_WIDTH // HEAD_DIM
NA_ROWS = 8
NA_COLS = 16

FN_CH = GROUP_WIDTH
FN_GROUPS = 4
FN_GROUP_CH = FN_CH // FN_GROUPS

D_FF = 4 * D_MODEL

ATT_Q_W = ATT_HEADS * HEAD_DIM
ATT_KV_W = ATT_KV_HEADS * HEAD_DIM
NA_W = NA_HEADS * HEAD_DIM
OFF_ATT_Q = 0
OFF_ATT_K = OFF_ATT_Q + ATT_Q_W
OFF_ATT_V = OFF_ATT_K + ATT_KV_W
OFF_S5 = OFF_ATT_V + ATT_KV_W
OFF_NA_Q = OFF_S5 + S5_CH
OFF_NA_K = OFF_NA_Q + NA_W
OFF_NA_V = OFF_NA_K + NA_W
OFF_FN = OFF_NA_V + NA_W
IN_WIDTH = OFF_FN + FN_CH
MIX_OUT = ATT_Q_W + S5_CH + NA_W + FN_CH

kernel_name = 'hybrid_parallel_group_dit_block'


def rms_norm(x, g):
    x32 = x.astype(jnp.float32)
    y = x32 * lax.rsqrt(jnp.mean(x32 * x32, axis=-1, keepdims=True) + EPS)
    return (y * g.astype(jnp.float32)).astype(x.dtype)


def modulate(x, shift, scale):
    return x * (1 + scale) + shift


def heads(t, lo, hi, nh):
    return t[..., lo:hi].reshape(t.shape[0], t.shape[1], nh, HEAD_DIM)


def rope_axis(x, pos):
    half = x.shape[-1] // 2
    inv = ROPE_BASE ** (-jnp.arange(half, dtype=jnp.float32) / half)
    ang = pos.astype(jnp.float32)[:, None] * inv[None, :]
    cos = jnp.cos(ang)[None, :, None, :]
    sin = jnp.sin(ang)[None, :, None, :]
    x1, x2 = x[..., :half], x[..., half:]
    return jnp.concatenate([x1 * cos - x2 * sin, x1 * sin + x2 * cos], axis=-1)


def rope_2d(x, row, col):
    x32 = x.astype(jnp.float32)
    r = x.shape[-1] // 2
    out = jnp.concatenate([rope_axis(x32[..., :r], row), rope_axis(x32[..., r:], col)], axis=-1)
    return out.astype(x.dtype)


def gqa_mixer(z, zc, g_q, g_k, row, col, need_ctx):
    b, n, _ = z.shape
    lc = zc.shape[1]
    scale = HEAD_DIM ** -0.5
    q = rope_2d(rms_norm(heads(z, OFF_ATT_Q, OFF_ATT_K, ATT_HEADS), g_q), row, col) * scale
    q = q.reshape(b, n, ATT_KV_HEADS, ATT_GROUP, HEAD_DIM)
    k = rope_2d(rms_norm(heads(z, OFF_ATT_K, OFF_ATT_V, ATT_KV_HEADS), g_k), row, col)
    v = heads(z, OFF_ATT_V, OFF_S5, ATT_KV_HEADS)
    kc = rms_norm(heads(zc, OFF_ATT_K, OFF_ATT_V, ATT_KV_HEADS), g_k)
    vc = heads(zc, OFF_ATT_V, OFF_S5, ATT_KV_HEADS)
    k_all = jnp.concatenate([k, kc], axis=1)
    v_all = jnp.concatenate([v, vc], axis=1)

    def attend(qi, kk, vv):
        s = jnp.einsum('bqkgd,bskd->bkgqs', qi, kk).astype(jnp.float32)
        p = jax.nn.softmax(s, axis=-1).astype(vv.dtype)
        return jnp.einsum('bkgqs,bskd->bqkgd', p, vv)

    nblk = n // Q_BLOCK
    qb = q.reshape(b, nblk, Q_BLOCK, ATT_KV_HEADS, ATT_GROUP, HEAD_DIM).transpose(1, 0, 2, 3, 4, 5)
    o = lax.map(lambda qi: attend(qi, k_all, v_all), qb)
    o = o.transpose(1, 0, 2, 3, 4, 5).reshape(b, n, ATT_Q_W)
    oc = None
    if need_ctx:
        qc = rms_norm(heads(zc, OFF_ATT_Q, OFF_ATT_K, ATT_HEADS), g_q) * scale
        qc = qc.reshape(b, lc, ATT_KV_HEADS, ATT_GROUP, HEAD_DIM)
        oc = attend(qc, kc, vc).reshape(b, lc, ATT_Q_W)
    return o, oc


def linear_scan(bu, abar, h0, reverse):
    if h0 is not None:
        idx = bu.shape[1] - 1 if reverse else 0
        bu = bu.at[:, idx].add(abar * h0)
    a = jnp.broadcast_to(abar, bu.shape)

    def combine(e1, e2):
        a1, b1 = e1
        a2, b2 = e2
        return a1 * a2, a2 * b1 + b2

    _, h = lax.associative_scan(combine, (a, bu), reverse=reverse, axis=1)
    return h


def s5_mixer(z, zc, a_re, a_im, log_dt, b_re, b_im, c_re, c_im, d_skip, w_glu, need_ctx):
    f32 = jnp.float32
    b, n, _ = z.shape
    lc = zc.shape[1]
    u = z[..., OFF_S5:OFF_NA_Q].astype(f32)
    uc = zc[..., OFF_S5:OFF_NA_Q].astype(f32)
    u_g = u.reshape(b, n, S5_GROUPS, S5_GROUP_CH).astype(jnp.complex64)
    uc_g = uc.reshape(b, lc, S5_GROUPS, S5_GROUP_CH).astype(jnp.complex64)
    d32 = d_skip.astype(f32)
    y = d32 * u
    yc = d32 * uc if need_ctx else None
    for direction in range(2):
        reverse = direction == 1
        lam = lax.complex(jnp.minimum(a_re[direction].astype(f32), -S5_MIN_DECAY), a_im[direction].astype(f32))
        dt = jnp.exp(log_dt[direction].astype(f32))[:, None]
        abar = jnp.exp(lam * dt)
        bmat = lax.complex(b_re[direction].astype(f32), b_im[direction].astype(f32))
        bbar = ((abar - 1) / lam)[..., None] * bmat
        cmat = lax.complex(c_re[direction].astype(f32), c_im[direction].astype(f32))
        hc = linear_scan(jnp.einsum('blgh,gph->blgp', uc_g, bbar), abar, None, reverse)
        h0 = hc[:, 0] if reverse else hc[:, -1]
        h = linear_scan(jnp.einsum('blgh,gph->blgp', u_g, bbar), abar, h0, reverse)
        y = y + jnp.einsum('blgp,ghp->blgh', h, cmat).real.reshape(b, n, S5_CH)
        if need_ctx:
            yc = yc + jnp.einsum('blgp,ghp->blgh', hc, cmat).real.reshape(b, lc, S5_CH)

    def glu(t):
        t = jax.nn.gelu(t).astype(z.dtype)
        return t * jax.nn.sigmoid(t @ w_glu)

    return glu(y), (glu(yc) if need_ctx else None)


def na_mixer(z, zc, rel_bias, need_ctx):
    b, n, _ = z.shape
    rows = n // GRID_W
    k_r = min(NA_ROWS, rows)
    scale = HEAD_DIM ** -0.5
    q = heads(z, OFF_NA_Q, OFF_NA_K, NA_HEADS) * scale
    k = heads(z, OFF_NA_K, OFF_NA_V, NA_HEADS)
    v = heads(z, OFF_NA_V, OFF_FN, NA_HEADS)
    kc = heads(zc, OFF_NA_K, OFF_NA_V, NA_HEADS)
    vc = heads(zc, OFF_NA_V, OFF_FN, NA_HEADS)
    q_grid = q.reshape(b, rows, GRID_W, NA_HEADS, HEAD_DIM)
    k_grid = k.reshape(b, rows, GRID_W, NA_HEADS, HEAD_DIM)
    v_grid = v.reshape(b, rows, GRID_W, NA_HEADS, HEAD_DIM)
    cols = jnp.arange(GRID_W)
    col_start = jnp.clip(cols - NA_COLS // 2, 0, GRID_W - NA_COLS)
    col_idx = col_start[:, None] + jnp.arange(NA_COLS)[None, :]
    rel_c = col_idx - cols[:, None] + (NA_COLS - 1)
    bias_c = rel_bias[:, :, rel_c]
    n_loc = k_r * NA_COLS

    def row_block(r):
        rs = jnp.clip(r - k_r // 2, 0, rows - k_r)
        qr = lax.dynamic_index_in_dim(q_grid, r, axis=1, keepdims=False)
        kb = lax.dynamic_slice_in_dim(k_grid, rs, k_r, axis=1)[:, :, col_idx]
        vb = lax.dynamic_slice_in_dim(v_grid, rs, k_r, axis=1)[:, :, col_idx]
        rel_r = rs + jnp.arange(k_r) - r + (NA_ROWS - 1)
        bias = bias_c[:, rel_r].transpose(0, 2, 1, 3).astype(jnp.float32)
        s_loc = jnp.einsum('bjhd,bajkhd->bhjak', qr, kb).astype(jnp.float32) + bias[None]
        s_loc = s_loc.reshape(b, NA_HEADS, GRID_W, n_loc)
        s_ctx = jnp.einsum('bjhd,bshd->bhjs', qr, kc).astype(jnp.float32)
        p = jax.nn.softmax(jnp.concatenate([s_loc, s_ctx], axis=-1), axis=-1).astype(vb.dtype)
        p_loc = p[..., :n_loc].reshape(b, NA_HEADS, GRID_W, k_r, NA_COLS)
        p_ctx = p[..., n_loc:]
        return jnp.einsum('bhjak,bajkhd->bjhd', p_loc, vb) + jnp.einsum('bhjs,bshd->bjhd', p_ctx, vc)

    o = lax.map(row_block, jnp.arange(rows))
    o = o.transpose(1, 0, 2, 3, 4).reshape(b, n, NA_W)
    oc = None
    if need_ctx:
        qc = heads(zc, OFF_NA_Q, OFF_NA_K, NA_HEADS) * scale
        s = jnp.einsum('bqhd,bshd->bhqs', qc, kc).astype(jnp.float32)
        p = jax.nn.softmax(s, axis=-1).astype(vc.dtype)
        oc = jnp.einsum('bhqs,bshd->bqhd', p, vc).reshape(zc.shape[0], zc.shape[1], NA_W)
    return o, oc


def fourier_mixer(t, w_fnet, b_fnet):
    b, n, _ = t.shape
    u = t[..., OFF_FN:IN_WIDTH].astype(jnp.float32).reshape(b, n, FN_GROUPS, FN_GROUP_CH)
    f = jnp.fft.fft2(u, axes=(1, 3), norm='ortho').real.reshape(b, n, FN_CH).astype(t.dtype)
    return f @ w_fnet + b_fnet


def sq_relu_mlp(h, w1, w2):
    return jnp.square(jax.nn.relu(h @ w1)) @ w2


def trunk_layer(x, xc, c, c_ctx, row, col, w_ada, b_ada, g_pre_mix, g_post_mix, g_pre_mlp, g_post_mlp,
                w_in, g_q_attn, g_k_attn, s5_a_re, s5_a_im, s5_log_dt, s5_b_re, s5_b_im, s5_c_re, s5_c_im,
                s5_d, w_s5_glu, na_rel_bias, w_fnet, b_fnet, w_out, w_mlp1, w_mlp2, need_ctx):
    b = x.shape[0]
    mod = (jax.nn.silu(c) @ w_ada + b_ada).reshape(b, N_MOD, D_MODEL)
    sh1, sc1, g1, sh2, sc2, g2 = [mod[:, i, None, :] for i in range(N_MOD)]
    mod_c = (jax.nn.silu(c_ctx) @ w_ada + b_ada).reshape(N_MOD, D_MODEL)
    csh1, csc1, cg1, csh2, csc2, cg2 = [mod_c[i] for i in range(N_MOD)]

    z = modulate(rms_norm(x, g_pre_mix), sh1, sc1) @ w_in
    zc = modulate(rms_norm(xc, g_pre_mix), csh1, csc1) @ w_in
    oa, oac = gqa_mixer(z, zc, g_q_attn, g_k_attn, row, col, need_ctx)
    ob, obc = s5_mixer(z, zc, s5_a_re, s5_a_im, s5_log_dt, s5_b_re, s5_b_im, s5_c_re, s5_c_im,
                       s5_d, w_s5_glu, need_ctx)
    on, onc = na_mixer(z, zc, na_rel_bias, need_ctx)
    od = fourier_mixer(z, w_fnet, b_fnet)
    y = jnp.concatenate([oa, ob, on, od], axis=-1) @ w_out
    x = x + g1 * rms_norm(y, g_post_mix)
    h = modulate(rms_norm(x, g_pre_mlp), sh2, sc2)
    x = x + g2 * rms_norm(sq_relu_mlp(h, w_mlp1, w_mlp2), g_post_mlp)

    if need_ctx:
        odc = fourier_mixer(zc, w_fnet, b_fnet)
        yc = jnp.concatenate([oac, obc, onc, odc], axis=-1) @ w_out
        xc = xc + cg1 * rms_norm(yc, g_post_mix)
        hc = modulate(rms_norm(xc, g_pre_mlp), csh2, csc2)
        xc = xc + cg2 * rms_norm(sq_relu_mlp(hc, w_mlp1, w_mlp2), g_post_mlp)
    return x, xc


def setup_inputs(seed: int = 0) -> dict:
    key = jax.random.key(seed)
    ks = jax.random.split(key, 32)
    f32 = jnp.float32
    L = DEPTH

    def nrm(k, shape, scale):
        return jax.random.normal(k, shape, f32) * scale

    n_idx = jnp.arange(S5_STATE, dtype=f32)
    s5_shape = (L, 2, S5_GROUPS, S5_STATE)
    return {
        'x': nrm(ks[0], (BATCH, SEQ, D_MODEL), 1.0),
        'c': nrm(ks[1], (BATCH, D_MODEL), 1.0),
        'ctx': nrm(ks[2], (BATCH, CTX_LEN, D_MODEL), 1.0),
        'c_ctx': nrm(ks[3], (D_MODEL,), 1.0),
        'w_ada': nrm(ks[4], (L, D_MODEL, N_MOD * D_MODEL), 0.5 * D_MODEL ** -0.5),
        'b_ada': nrm(ks[5], (L, N_MOD * D_MODEL), 0.01),
        'g_pre_mix': 1.0 + nrm(ks[6], (L, D_MODEL), 0.02),
        'g_post_mix': 1.0 + nrm(ks[7], (L, D_MODEL), 0.02),
        'g_pre_mlp': 1.0 + nrm(ks[8], (L, D_MODEL), 0.02),
        'g_post_mlp': 1.0 + nrm(ks[9], (L, D_MODEL), 0.02),
        'w_in': nrm(ks[10], (L, D_MODEL, IN_WIDTH), D_MODEL ** -0.5),
        'g_q_attn': 1.0 + nrm(ks[11], (L, HEAD_DIM), 0.02),
        'g_k_attn': 1.0 + nrm(ks[12], (L, HEAD_DIM), 0.02),
        's5_a_re': -0.5 + nrm(ks[13], s5_shape, 0.01),
        's5_a_im': math.pi * n_idx + nrm(ks[14], s5_shape, 0.01),
        's5_log_dt': jax.random.uniform(ks[15], (L, 2, S5_GROUPS), f32, math.log(S5_DT_MIN), math.log(S5_DT_MAX)),
        's5_b_re': nrm(ks[16], (L, 2, S5_GROUPS, S5_STATE, S5_GROUP_CH), (2 * S5_GROUP_CH) ** -0.5),
        's5_b_im': nrm(ks[17], (L, 2, S5_GROUPS, S5_STATE, S5_GROUP_CH), (2 * S5_GROUP_CH) ** -0.5),
        's5_c_re': nrm(ks[18], (L, 2, S5_GROUPS, S5_GROUP_CH, S5_STATE), S5_STATE ** -0.5),
        's5_c_im': nrm(ks[19], (L, 2, S5_GROUPS, S5_GROUP_CH, S5_STATE), S5_STATE ** -0.5),
        's5_d': nrm(ks[20], (L, S5_CH), 1.0),
        'w_s5_glu': nrm(ks[21], (L, S5_CH, S5_CH), S5_CH ** -0.5),
        'na_rel_bias': nrm(ks[22], (L, NA_HEADS, 2 * NA_ROWS - 1, 2 * NA_COLS - 1), 0.02),
        'w_fnet': nrm(ks[23], (L, FN_CH, FN_CH), FN_CH ** -0.5),
        'b_fnet': nrm(ks[24], (L, FN_CH), 0.01),
        'w_out': nrm(ks[25], (L, MIX_OUT, D_MODEL), MIX_OUT ** -0.5),
        'w_mlp1': nrm(ks[26], (L, D_MODEL, D_FF), D_MODEL ** -0.5),
        'w_mlp2': nrm(ks[27], (L, D_FF, D_MODEL), D_FF ** -0.5),
    }


def reference(x, c, ctx, c_ctx, w_ada, b_ada, g_pre_mix, g_post_mix, g_pre_mlp, g_post_mlp, w_in,
              g_q_attn, g_k_attn, s5_a_re, s5_a_im, s5_log_dt, s5_b_re, s5_b_im, s5_c_re, s5_c_im,
              s5_d, w_s5_glu, na_rel_bias, w_fnet, b_fnet, w_out, w_mlp1, w_mlp2):
    n = x.shape[1]
    pos = jnp.arange(n, dtype=jnp.int32)
    row = pos // GRID_W
    col = pos % GRID_W
    xc = ctx
    for l in range(DEPTH):
        need_ctx = l < DEPTH - 1
        x, xc = trunk_layer(x, xc, c, c_ctx, row, col, w_ada[l], b_ada[l], g_pre_mix[l], g_post_mix[l],
                            g_pre_mlp[l], g_post_mlp[l], w_in[l], g_q_attn[l], g_k_attn[l],
                            s5_a_re[l], s5_a_im[l], s5_log_dt[l], s5_b_re[l], s5_b_im[l], s5_c_re[l], s5_c_im[l],
                            s5_d[l], w_s5_glu[l], na_rel_bias[l], w_fnet[l], b_fnet[l], w_out[l],
                            w_mlp1[l], w_mlp2[l], need_ctx)
    return x
```

```python
import functools
import math

import jax
import jax.numpy as jnp
from jax import lax
from jax.experimental import pallas as pl
from jax.experimental.pallas import tpu as pltpu

F32 = jnp.float32
BF16 = jnp.bfloat16
HIGHEST = lax.Precision.HIGHEST

HEAD_DIM = 64
GRID_W = 64
N_MOD = 6
EPS = 1e-6
ATT_HEADS = 4
ATT_KV_HEADS = 2
ROPE_BASE = 10000.0
S5_GROUPS = 16
S5_GROUP_CH = 16
S5_STATE = 64
S5_MIN_DECAY = 1e-4
NA_HEADS = 4
NA_ROWS = 8
NA_COLS = 16
FN_GROUPS = 4
GW = 256
MASK_VALUE = -1e30

OFF_ATT_Q, OFF_ATT_K, OFF_ATT_V, OFF_S5 = 0, 256, 384, 512
OFF_NA_Q, OFF_NA_K, OFF_NA_V, OFF_FN, IN_WIDTH = 768, 1024, 1280, 1536, 1792

VMEM_LIMIT = 56 * 1024 * 1024


def _params(*sem):
    return pltpu.CompilerParams(dimension_semantics=sem, vmem_limit_bytes=VMEM_LIMIT)


def _const_spec(shape):
    return pl.BlockSpec(shape, lambda *_: (0,) * len(shape), pipeline_mode=pl.Buffered(1))


def _rms(x, g):
    return x * lax.rsqrt(jnp.mean(x * x, axis=-1, keepdims=True) + EPS) * g


def _nt_dot(a, b):
    return lax.dot_general(a, b, (((1,), (1,)), ((), ())), preferred_element_type=F32)


def _ada_body(c_ref, w_ref, b_ref, o_ref):
    c = c_ref[...]
    s = c * jax.nn.sigmoid(c)
    o_ref[...] = jnp.dot(s, w_ref[...], precision=HIGHEST, preferred_element_type=F32) + b_ref[...]


def _ada(cc, w_ada, b_ada, tn=1536):
    depth, d, n_out = w_ada.shape
    rows = cc.shape[0]
    return pl.pallas_call(
        _ada_body,
        grid=(depth, n_out // tn),
        in_specs=[
            pl.BlockSpec((rows, d), lambda l, j: (0, 0)),
            pl.BlockSpec((None, d, tn), lambda l, j: (l, 0, j)),
            pl.BlockSpec((None, 1, tn), lambda l, j: (l, 0, j)),
        ],
        out_specs=pl.BlockSpec((None, rows, tn), lambda l, j: (l, 0, j)),
        out_shape=jax.ShapeDtypeStruct((depth, rows, n_out), F32),
        compiler_params=_params("parallel", "parallel"),
        name="ada",
    )(cc, w_ada, b_ada.reshape(depth, 1, n_out))


def _s5_prep_body(are_ref, aim_ref, ldt_ref, are_b_ref, aim_b_ref, ldt_b_ref, bre_ref, bim_ref,
                  abr_ref, abi_ref, bbr_ref, bbi_ref):
    def zoh(a_re, a_im, log_dt):
        lr = jnp.minimum(a_re, -S5_MIN_DECAY)
        dt = jnp.exp(log_dt)
        mag = jnp.exp(lr * dt)
        ab_r = mag * jnp.cos(a_im * dt)
        ab_i = mag * jnp.sin(a_im * dt)
        return lr, ab_r, ab_i

    _, ab_r, ab_i = zoh(are_ref[...], aim_ref[...], ldt_ref[...])
    abr_ref[...] = ab_r
    abi_ref[...] = ab_i
    lr, ab_r, ab_i = zoh(are_b_ref[...], aim_b_ref[...], ldt_b_ref[...])
    li = aim_b_ref[...]
    nr, ni = ab_r - 1.0, ab_i
    den = lr * lr + li * li
    kr = (nr * lr + ni * li) / den
    ki = (ni * lr - nr * li) / den
    br, bi = bre_ref[...], bim_ref[...]
    bbr_ref[...] = kr * br - ki * bi
    bbi_ref[...] = kr * bi + ki * br


def _s5_prep(a_re, a_im, log_dt, b_re, b_im):
    lead = a_re.shape[:3]
    r = lead[0] * lead[1] * lead[2]
    p, h = S5_STATE, S5_GROUP_CH
    a2 = lambda t: t.reshape(r, p)
    ab = lambda t: jnp.broadcast_to(t.reshape(r, p, 1), (r, p, h)).reshape(r, p * h)
    ldt = jnp.broadcast_to(log_dt.reshape(r, 1), (r, p))
    ldt_b = jnp.broadcast_to(log_dt.reshape(r, 1), (r, p * h))
    small = jax.ShapeDtypeStruct((r, p), F32)
    big = jax.ShapeDtypeStruct((r, p * h), F32)
    abr, abi, bbr, bbi = pl.pallas_call(
        _s5_prep_body, out_shape=(small, small, big, big), name="s5_prep",
    )(a2(a_re), a2(a_im), ldt, ab(a_re), ab(a_im), ldt_b, b_re.reshape(r, p * h), b_im.reshape(r, p * h))
    return (abr.reshape(*lead, p), abi.reshape(*lead, p),
            bbr.reshape(*lead, p, h), bbi.reshape(*lead, p, h))


def _in_proj_body(*refs, rope):
    if rope:
        (x_ref, sh_ref, sc_ref, g_ref, w_ref, gq_ref, gk_ref, hm_ref, cos_ref, sa_ref, sb_ref,
         qa_ref, ka_ref, va_ref, us_ref, nq_ref, nk_ref, nv_ref, fn_ref) = refs
    else:
        (x_ref, sh_ref, sc_ref, g_ref, w_ref, gq_ref, gk_ref, hm_ref,
         qa_ref, ka_ref, va_ref, us_ref, nq_ref, nk_ref, nv_ref, fn_ref) = refs
    h = _rms(x_ref[...], g_ref[...]) * (1.0 + sc_ref[...]) + sh_ref[...]
    z = jnp.dot(h.astype(BF16), w_ref[...], preferred_element_type=F32)
    hm = hm_ref[...]

    def head_norm(t, g, avg):
        ms = jnp.dot(t * t, avg, precision=HIGHEST, preferred_element_type=F32)
        return t * lax.rsqrt(ms + EPS) * g

    q = head_norm(z[:, OFF_ATT_Q:OFF_ATT_K], gq_ref[...], hm)
    kw = OFF_ATT_V - OFF_ATT_K
    k = head_norm(z[:, OFF_ATT_K:OFF_ATT_V], gk_ref[...], hm[:kw, :kw])
    if rope:
        def rot(t):
            w = t.shape[-1]
            half = HEAD_DIM // 4
            return (t * cos_ref[:, :w] + pltpu.roll(t, w - half, 1) * sa_ref[:, :w]
                    + pltpu.roll(t, half, 1) * sb_ref[:, :w])
        q, k = rot(q), rot(k)
    scale = HEAD_DIM ** -0.5
    qa_ref[...] = (q * scale).astype(qa_ref.dtype)
    ka_ref[...] = k.astype(ka_ref.dtype)
    va_ref[...] = z[:, OFF_ATT_V:OFF_S5].astype(va_ref.dtype)
    us_ref[...] = z[:, OFF_S5:OFF_NA_Q].astype(us_ref.dtype)
    nq_ref[...] = (z[:, OFF_NA_Q:OFF_NA_K] * scale).astype(nq_ref.dtype)
    nk_ref[...] = z[:, OFF_NA_K:OFF_NA_V].astype(nk_ref.dtype)
    nv_ref[...] = z[:, OFF_NA_V:OFF_FN].astype(nv_ref.dtype)
    fn_ref[...] = z[:, OFF_FN:IN_WIDTH].astype(fn_ref.dtype)


def _in_proj(x, sh, sc, g_pre, w_in, gq, gk, hm, rope_tabs, tm):
    b, n, d = x.shape
    rope = rope_tabs is not None
    row = lambda w: pl.BlockSpec((None, tm, w), lambda bi, i: (bi, i, 0))
    mod = pl.BlockSpec((None, 1, d), lambda bi, i: (bi, 0, 0))
    in_specs = [row(d), mod, mod, _const_spec((1, d)), _const_spec(w_in.shape),
                _const_spec(gq.shape), _const_spec(gk.shape), _const_spec(hm.shape)]
    args = [x, sh, sc, g_pre, w_in, gq, gk, hm]
    if rope:
        in_specs += [pl.BlockSpec((tm, GW), lambda bi, i: (i, 0))] * 3
        args += list(rope_tabs)
    kvw = ATT_KV_HEADS * HEAD_DIM
    widths = [GW, kvw, kvw, None, GW, GW, GW, GW]
    out_specs = [pl.BlockSpec((tm, GW), lambda bi, i: (i, bi)) if w is None else row(w) for w in widths]
    out_shape = [jax.ShapeDtypeStruct((n, b * GW), BF16) if w is None else jax.ShapeDtypeStruct((b, n, w), BF16)
                 for w in widths]
    return pl.pallas_call(
        functools.partial(_in_proj_body, rope=rope),
        grid=(b, n // tm), in_specs=in_specs, out_specs=out_specs, out_shape=out_shape,
        compiler_params=_params("parallel", "parallel"), name="in_proj_rope" if rope else "in_proj",
    )(*args)


def _mha_body(*refs, n_heads, group, two_sets):
    if two_sets:
        q_ref, k1_ref, v1_ref, k2_ref, v2_ref, o_ref = refs
    else:
        q_ref, k1_ref, v1_ref, o_ref = refs
    for h in range(n_heads):
        qs = slice(HEAD_DIM * h, HEAD_DIM * (h + 1))
        ks = slice(HEAD_DIM * (h // group), HEAD_DIM * (h // group + 1))
        qh = q_ref[:, qs]
        s1 = _nt_dot(qh, k1_ref[:, ks])
        m = jnp.max(s1, axis=-1, keepdims=True)
        if two_sets:
            s2 = _nt_dot(qh, k2_ref[:, ks])
            m = jnp.maximum(m, jnp.max(s2, axis=-1, keepdims=True))
        p1 = jnp.exp(s1 - m)
        l = jnp.sum(p1, axis=-1, keepdims=True)
        o = jnp.dot(p1.astype(BF16), v1_ref[:, ks], preferred_element_type=F32)
        if two_sets:
            p2 = jnp.exp(s2 - m)
            l = l + jnp.sum(p2, axis=-1, keepdims=True)
            o = o + jnp.dot(p2.astype(BF16), v2_ref[:, ks], preferred_element_type=F32)
        o_ref[:, qs] = (o / l).astype(o_ref.dtype)


def _mha(q, k1, v1, k2=None, v2=None, *, group, tq):
    b, nq, qw = q.shape
    two_sets = k2 is not None
    full = lambda t: pl.BlockSpec((None,) + t.shape[1:], lambda bi, i: (bi, 0, 0))
    args = [q, k1, v1] + ([k2, v2] if two_sets else [])
    in_specs = [pl.BlockSpec((None, tq, qw), lambda bi, i: (bi, i, 0))] + [full(t) for t in args[1:]]
    return pl.pallas_call(
        functools.partial(_mha_body, n_heads=qw // HEAD_DIM, group=group, two_sets=two_sets),
        grid=(b, nq // tq), in_specs=in_specs,
        out_specs=pl.BlockSpec((None, tq, qw), lambda bi, i: (bi, i, 0)),
        out_shape=jax.ShapeDtypeStruct((b, nq, qw), BF16),
        compiler_params=_params("parallel", "parallel"), name="mha2" if two_sets else "mha1",
    )(*args)


def _na_body(q_ref, k_ref, v_ref, kc_ref, vc_ref, bias_ref, o_ref, *, rt, k_r, rows):
    i = pl.program_id(1)
    kc = kc_ref[...]
    vc = vc_ref[...]
    lane_head = lax.broadcasted_iota(jnp.int32, (GRID_W, GW), 1) // HEAD_DIM
    for j in range(rt):
        r = i * rt + j
        rs = jnp.clip(r - k_r // 2, 0, rows - k_r)
        start = pl.multiple_of(rs * GRID_W, GRID_W)
        kw = k_ref[pl.ds(start, k_r * GRID_W), :]
        vw = v_ref[pl.ds(start, k_r * GRID_W), :]
        q = q_ref[j * GRID_W:(j + 1) * GRID_W, :]
        q4 = jnp.concatenate([jnp.where(lane_head == h, q, jnp.zeros_like(q)) for h in range(NA_HEADS)], axis=0)
        s_loc = _nt_dot(q4, kw) + bias_ref[r - rs]
        s_ctx = _nt_dot(q4, kc)
        m = jnp.maximum(jnp.max(s_loc, axis=-1, keepdims=True), jnp.max(s_ctx, axis=-1, keepdims=True))
        p_loc = jnp.exp(s_loc - m)
        p_ctx = jnp.exp(s_ctx - m)
        l = jnp.sum(p_loc, axis=-1, keepdims=True) + jnp.sum(p_ctx, axis=-1, keepdims=True)
        o4 = (jnp.dot(p_loc.astype(BF16), vw, preferred_element_type=F32)
              + jnp.dot(p_ctx.astype(BF16), vc, preferred_element_type=F32)) / l
        o = jnp.zeros((GRID_W, GW), F32)
        for h in range(NA_HEADS):
            o = o + jnp.where(lane_head == h, o4[h * GRID_W:(h + 1) * GRID_W, :], 0.0)
        o_ref[j * GRID_W:(j + 1) * GRID_W, :] = o.astype(o_ref.dtype)


def _na(q, k, v, kc, vc, bias, rt):
    b, n, _ = q.shape
    rows = n // GRID_W
    k_r = bias.shape[0]
    full = lambda t: pl.BlockSpec((None,) + t.shape[1:], lambda bi, i: (bi, 0, 0))
    return pl.pallas_call(
        functools.partial(_na_body, rt=rt, k_r=k_r, rows=rows),
        grid=(b, rows // rt),
        in_specs=[pl.BlockSpec((None, rt * GRID_W, GW), lambda bi, i: (bi, i, 0)),
                  full(k), full(v), full(kc), full(vc), _const_spec(bias.shape)],
        out_specs=pl.BlockSpec((None, rt * GRID_W, GW), lambda bi, i: (bi, i, 0)),
        out_shape=jax.ShapeDtypeStruct((b, n, GW), BF16),
        compiler_params=_params("parallel", "parallel"), name="na",
    )(q, k, v, kc, vc, bias)


def _na_bias_table(rel_bias, rows):
    k_r = min(NA_ROWS, rows)
    cols = jnp.arange(GRID_W)
    col_start = jnp.clip(cols - NA_COLS // 2, 0, GRID_W - NA_COLS)
    ck = jnp.arange(GRID_W)
    inside = (ck[None, :] >= col_start[:, None]) & (ck[None, :] < col_start[:, None] + NA_COLS)
    rel_c = jnp.clip(ck[None, :] - cols[:, None] + (NA_COLS - 1), 0, 2 * NA_COLS - 2)
    cls = jnp.arange(k_r)
    a = jnp.arange(k_r)
    rel_r = a[None, :] - cls[:, None] + (NA_ROWS - 1)
    t = rel_bias.astype(F32)[:, rel_r[:, :, None, None], rel_c[None, None, :, :]]
    t = jnp.where(inside[None, None, None], t, MASK_VALUE)
    t = t.transpose(1, 0, 3, 2, 4)
    return t.reshape(k_r, NA_HEADS * GRID_W, k_r * GRID_W)


def _s5_body(u_ref, bd_ref, cd_ref, a_ref, y_ref, hs_ref, st_ref, *, tc, nb, n_state, col_w):
    d = pl.program_id(0)

    @pl.when(pl.program_id(1) == 0)
    def _():
        st_ref[...] = jnp.zeros_like(st_ref)

    hs_ref[...] = jnp.dot(u_ref[...], bd_ref[...], preferred_element_type=F32)
    for c0 in range(0, n_state, col_w):
        re = slice(c0, c0 + col_w)
        im = slice(n_state + c0, n_state + c0 + col_w)
        ar = jnp.broadcast_to(a_ref[:, re], (nb, col_w))
        ai = jnp.broadcast_to(a_ref[:, im], (nb, col_w))

        def step(t, carry):
            hr, hi = carry
            te = jnp.where(d == 0, t, tc - 1 - t)
            rows = pl.ds(pl.multiple_of(te * nb, nb), nb)
            nr = ar * hr - ai * hi + hs_ref[rows, re]
            ni = ar * hi + ai * hr + hs_ref[rows, im]
            hs_ref[rows, re] = nr
            hs_ref[rows, im] = ni
            return nr, ni

        hr, hi = lax.fori_loop(0, tc, step, (st_ref[:, re], st_ref[:, im]), unroll=2)
        st_ref[:, re] = hr
        st_ref[:, im] = hi
    y_ref[...] = jnp.dot(hs_ref[...].astype(BF16), cd_ref[...], preferred_element_type=F32)


def _s5(u_tm, bd, cd, a_row, nb, lc, tc):
    rows_total = u_tm.shape[0]
    t_total = rows_total // nb
    n_c, n_all = lc // tc, t_total // tc
    n_state2 = bd.shape[-1]

    def blk(d, j):
        back = jnp.where(j < n_c, n_c - 1 - j, n_c + (n_all - 1 - j))
        return jnp.where(d == 0, j, back)

    return pl.pallas_call(
        functools.partial(_s5_body, tc=tc, nb=nb, n_state=n_state2 // 2, col_w=512),
        grid=(2, n_all),
        in_specs=[pl.BlockSpec((tc * nb, GW), lambda d, j: (blk(d, j), 0)),
                  pl.BlockSpec((None, GW, n_state2), lambda d, j: (d, 0, 0)),
                  pl.BlockSpec((None, n_state2, GW), lambda d, j: (d, 0, 0)),
                  pl.BlockSpec((None, 1, n_state2), lambda d, j: (d, 0, 0))],
        out_specs=pl.BlockSpec((None, tc * nb, GW), lambda d, j: (d, blk(d, j), 0)),
        out_shape=jax.ShapeDtypeStruct((2, rows_total, GW), F32),
        scratch_shapes=[pltpu.VMEM((tc * nb, n_state2), F32), pltpu.VMEM((nb, n_state2), F32)],
        compiler_params=_params("arbitrary", "arbitrary"), name="s5",
    )(u_tm, bd, cd, a_row)


def _s5_matrices(abr, abi, bbr, bbi, c_re, c_im):
    g, p, h = S5_GROUPS, S5_STATE, S5_GROUP_CH
    eye = jnp.eye(g, dtype=F32)
    to_bd = lambda t: jnp.einsum("dgph,gk->dghkp", t, eye).reshape(2, g * h, g * p)
    bd = jnp.concatenate([to_bd(bbr), to_bd(bbi)], axis=-1)
    to_cd = lambda t: jnp.einsum("dghp,gk->dgpkh", t, eye).reshape(2, g * p, g * h)
    cd = jnp.concatenate([to_cd(c_re.astype(F32)), to_cd(-c_im.astype(F32))], axis=1)
    a_row = jnp.concatenate([abr.reshape(2, 1, g * p), abi.reshape(2, 1, g * p)], axis=-1)
    return bd.astype(BF16), cd.astype(BF16), a_row


def _fnet_body(u_ref, dft_ref, ccs_ref, w_ref, b_ref, o_ref, uc_ref):
    n = u_ref.shape[0]

    @pl.when(pl.program_id(1) == 0)
    def _():
        t = jnp.dot(u_ref[...], ccs_ref[...], preferred_element_type=F32)
        uc_ref[0:n, :] = t[:, :GW].astype(BF16)
        uc_ref[n:2 * n, :] = t[:, GW:].astype(BF16)

    f = jnp.dot(dft_ref[...], uc_ref[...], preferred_element_type=F32)
    o = jnp.dot(f.astype(BF16), w_ref[...], preferred_element_type=F32) + b_ref[...]
    o_ref[...] = o.astype(o_ref.dtype)


def _fnet(u, dft, ccs, w, bias, tk):
    b, n, _ = u.shape
    return pl.pallas_call(
        _fnet_body, grid=(b, n // tk),
        in_specs=[pl.BlockSpec((None, n, GW), lambda bi, k: (bi, 0, 0)),
                  pl.BlockSpec((tk, 2 * n), lambda bi, k: (k, 0)),
                  _const_spec(ccs.shape), _const_spec(w.shape), _const_spec(bias.shape)],
        out_specs=pl.BlockSpec((None, tk, GW), lambda bi, k: (bi, k, 0)),
        out_shape=jax.ShapeDtypeStruct((b, n, GW), BF16),
        scratch_shapes=[pltpu.VMEM((2 * n, GW), BF16)],
        compiler_params=_params("parallel", "arbitrary"), name="fnet",
    )(u, dft, ccs, w, bias)


def _dft_matrices(n):
    ch = GW // FN_GROUPS
    k = jnp.arange(n)
    ang = (2.0 * math.pi / n) * ((k[:, None] * k[None, :]) % n).astype(F32)
    dft = jnp.concatenate([jnp.cos(ang), -jnp.sin(ang)], axis=1).astype(BF16)
    c = jnp.arange(GW)
    same = (c[:, None] // ch) == (c[None, :] // ch)
    angc = (2.0 * math.pi / ch) * (((c[:, None] % ch) * (c[None, :] % ch)) % ch).astype(F32)
    norm = 1.0 / math.sqrt(n * ch)
    cc = jnp.where(same, jnp.cos(angc), 0.0) * norm
    sc = jnp.where(same, jnp.sin(angc), 0.0) * norm
    return dft, jnp.concatenate([cc, sc], axis=1).astype(BF16)


def _out_mlp_body(x_ref, oa_ref, yf_ref, yb_ref, us_ref, on_ref, od_ref, g1_ref, sh2_ref, sc2_ref, g2_ref,
                  dsk_ref, wglu_ref, wout_ref, gpm_ref, gpre_ref, gpost_ref, w1_ref, w2_ref, o_ref, *, ff_chunk):
    t = dsk_ref[...] * us_ref[...].astype(F32) + yf_ref[...] + yb_ref[...]
    g = jax.nn.gelu(t)
    ob = g * jax.nn.sigmoid(jnp.dot(g.astype(BF16), wglu_ref[...], preferred_element_type=F32))
    cat = jnp.concatenate([oa_ref[...], ob.astype(BF16), on_ref[...], od_ref[...]], axis=-1)
    y = jnp.dot(cat, wout_ref[...], preferred_element_type=F32)
    x1 = x_ref[...] + g1_ref[...] * _rms(y, gpm_ref[...])
    h = (_rms(x1, gpre_ref[...]) * (1.0 + sc2_ref[...]) + sh2_ref[...]).astype(BF16)
    d_ff = w1_ref.shape[1]
    m = jnp.zeros(x1.shape, F32)
    for c0 in range(0, d_ff, ff_chunk):
        a = jnp.maximum(jnp.dot(h, w1_ref[:, c0:c0 + ff_chunk], preferred_element_type=F32), 0.0)
        m = m + jnp.dot((a * a).astype(BF16), w2_ref[c0:c0 + ff_chunk, :], preferred_element_type=F32)
    o_ref[...] = x1 + g2_ref[...] * _rms(m, gpost_ref[...])


def _out_mlp(x, oa, yd, t_off, us_tm, on, od, mods, dsk, wglu, wout, gpm, gpre, gpost, w1, w2, tm):
    b, n, d = x.shape
    off = t_off // tm
    row = lambda w: pl.BlockSpec((None, tm, w), lambda bi, i: (bi, i, 0))
    mod = pl.BlockSpec((None, 1, d), lambda bi, i: (bi, 0, 0))
    ydir = lambda dd: pl.BlockSpec((None, tm, GW), lambda bi, i: (dd, off + i, bi))
    consts = [dsk, wglu, wout, gpm, gpre, gpost, w1, w2]
    return pl.pallas_call(
        functools.partial(_out_mlp_body, ff_chunk=1024),
        grid=(b, n // tm),
        in_specs=[row(d), row(GW), ydir(0), ydir(1), pl.BlockSpec((tm, GW), lambda bi, i: (i, bi)),
                  row(GW), row(GW), mod, mod, mod, mod] + [_const_spec(t.shape) for t in consts],
        out_specs=row(d), out_shape=jax.ShapeDtypeStruct((b, n, d), F32),
        compiler_params=_params("parallel", "parallel"), name="out_mlp",
    )(x, oa, yd, yd, us_tm, on, od, *mods, *consts)


def _rope_tables(n):
    half = HEAD_DIM // 4
    pos = jnp.arange(n, dtype=jnp.int32)
    row = (pos // GRID_W).astype(F32)
    col = (pos % GRID_W).astype(F32)
    lane = jnp.arange(GW)
    jj = lane % HEAD_DIM
    inv = ROPE_BASE ** (-(jj % half).astype(F32) / half)
    p = jnp.where((jj // (2 * half) == 0)[None, :], row[:, None], col[:, None])
    ang = p * inv[None, :]
    first = ((jj % (2 * half)) < half)[None, :]
    sin = jnp.sin(ang)
    return jnp.cos(ang), jnp.where(first, -sin, 0.0), jnp.where(first, 0.0, sin)


def kernel(x, c, ctx, c_ctx, w_ada, b_ada, g_pre_mix, g_post_mix, g_pre_mlp, g_post_mlp, w_in, g_q_attn, g_k_attn,
           s5_a_re, s5_a_im, s5_log_dt, s5_b_re, s5_b_im, s5_c_re, s5_c_im, s5_d, w_s5_glu, na_rel_bias,
           w_fnet, b_fnet, w_out, w_mlp1, w_mlp2):
    b, n, d = x.shape
    lc = ctx.shape[1]
    depth = w_ada.shape[0]
    rows = n // GRID_W

    n_rows = -(-(b + 1) // 8) * 8
    cc = jnp.concatenate([c, c_ctx[None, :], jnp.zeros((n_rows - b - 1, d), c.dtype)], axis=0)
    mod_all = _ada(cc, w_ada, b_ada).reshape(depth, n_rows, N_MOD, d)

    abr, abi, bbr, bbi = _s5_prep(s5_a_re, s5_a_im, s5_log_dt, s5_b_re, s5_b_im)
    rope_tabs = _rope_tables(n)
    lane = jnp.arange(GW)
    hm = jnp.where((lane[:, None] // HEAD_DIM) == (lane[None, :] // HEAD_DIM), 1.0 / HEAD_DIM, 0.0).astype(F32)
    dft_n, ccs_n = _dft_matrices(n)
    dft_c, ccs_c = _dft_matrices(lc)

    tm_in = min(512, n)
    tm_c = min(256, lc)
    tm_out = math.gcd(256, math.gcd(n, lc))
    tc = math.gcd(64, math.gcd(n, lc))

    xc = ctx
    for l in range(depth):
        need_ctx = l < depth - 1
        mods = [mod_all[l, :b, i][:, None, :] for i in range(N_MOD)]
        mods_c = [jnp.broadcast_to(mod_all[l, b, i][None, None, :], (b, 1, d)) for i in range(N_MOD)]
        vec = lambda t: t[l].reshape(1, -1).astype(F32)
        w_in_l = w_in[l].astype(BF16)
        gq = jnp.tile(g_q_attn[l].astype(F32), ATT_HEADS)[None, :]
        gk = jnp.tile(g_k_attn[l].astype(F32), ATT_KV_HEADS)[None, :]

        qa, ka, va, us, nq, nk, nv, fn = _in_proj(x, mods[0], mods[1], vec(g_pre_mix), w_in_l, gq, gk, hm,
                                                  rope_tabs, tm_in)
        qac, kac, vac, usc, nqc, nkc, nvc, fnc = _in_proj(xc, mods_c[0], mods_c[1], vec(g_pre_mix), w_in_l,
                                                          gq, gk, hm, None, tm_c)

        oa = _mha(qa, ka, va, kac, vac, group=ATT_HEADS // ATT_KV_HEADS, tq=min(256, n))
        on = _na(nq, nk, nv, nkc, nvc, _na_bias_table(na_rel_bias[l], rows), rt=min(4, rows))
        w_fn = w_fnet[l].astype(BF16)
        od = _fnet(fn, dft_n, ccs_n, w_fn, vec(b_fnet), tk=min(512, n))

        bd, cd, a_row = _s5_matrices(abr[l], abi[l], bbr[l], bbi[l], s5_c_re[l], s5_c_im[l])
        u_tm = jnp.concatenate([usc, us], axis=0).reshape((lc + n) * b, GW)
        yd = _s5(u_tm, bd, cd, a_row, b, lc, tc).reshape(2, lc + n, b * GW)

        consts = (vec(s5_d), w_s5_glu[l].astype(BF16), w_out[l].astype(BF16), vec(g_post_mix), vec(g_pre_mlp),
                  vec(g_post_mlp), w_mlp1[l].astype(BF16), w_mlp2[l].astype(BF16))
        x_new = _out_mlp(x, oa, yd, lc, us, on, od, (mods[2], mods[3], mods[4], mods[5]), *consts, tm_out)
        if need_ctx:
            oac = _mha(qac, kac, vac, group=ATT_HEADS // ATT_KV_HEADS, tq=tm_c)
            onc = _mha(nqc, nkc, nvc, group=1, tq=tm_c)
            odc = _fnet(fnc, dft_c, ccs_c, w_fn, vec(b_fnet), tk=tm_c)
            xc = _out_mlp(xc, oac, yd, 0, usc, onc, odc, (mods_c[2], mods_c[3], mods_c[4], mods_c[5]),
                          *consts, tm_out)
        x = x_new
    return x
```

```python
import functools
import math

import jax
import jax.numpy as jnp
import numpy as np
from jax import lax
from jax.experimental import pallas as pl
from jax.experimental.pallas import tpu as pltpu

F32 = jnp.float32
BF16 = jnp.bfloat16
HIGHEST = lax.Precision.HIGHEST

HEAD_DIM = 64
GRID_W = 64
N_MOD = 6
EPS = 1e-6
ATT_HEADS = 4
ATT_KV_HEADS = 2
ROPE_BASE = 10000.0
S5_GROUPS = 16
S5_GROUP_CH = 16
S5_STATE = 64
S5_MIN_DECAY = 1e-4
NA_HEADS = 4
NA_ROWS = 8
NA_COLS = 16
FN_GROUPS = 4
GW = 256
MASK_VALUE = -1e30

OFF_ATT_Q, OFF_ATT_K, OFF_ATT_V, OFF_S5 = 0, 256, 384, 512
OFF_NA_Q, OFF_NA_K, OFF_NA_V, OFF_FN, IN_WIDTH = 768, 1024, 1280, 1536, 1792

VMEM_LIMIT = 56 * 1024 * 1024


def _params(*sem):
    return pltpu.CompilerParams(dimension_semantics=sem, vmem_limit_bytes=VMEM_LIMIT)


def _const_spec(shape):
    return pl.BlockSpec(shape, lambda *_: (0,) * len(shape), pipeline_mode=pl.Buffered(1))


def _rms(x, g):
    return x * lax.rsqrt(jnp.mean(x * x, axis=-1, keepdims=True) + EPS) * g


def _nt_dot(a, b):
    return lax.dot_general(a, b, (((1,), (1,)), ((), ())), preferred_element_type=F32)


def _ada_body(c_ref, w_ref, b_ref, o_ref):
    c = c_ref[...]
    s = c * jax.nn.sigmoid(c)
    o_ref[...] = jnp.dot(s, w_ref[...], precision=HIGHEST, preferred_element_type=F32) + b_ref[...]


def _ada(cc, w_ada, b_ada, tn=1536):
    depth, d, n_out = w_ada.shape
    rows = cc.shape[0]
    return pl.pallas_call(
        _ada_body,
        grid=(depth, n_out // tn),
        in_specs=[
            pl.BlockSpec((rows, d), lambda l, j: (0, 0)),
            pl.BlockSpec((None, d, tn), lambda l, j: (l, 0, j)),
            pl.BlockSpec((None, 1, tn), lambda l, j: (l, 0, j)),
        ],
        out_specs=pl.BlockSpec((None, rows, tn), lambda l, j: (l, 0, j)),
        out_shape=jax.ShapeDtypeStruct((depth, rows, n_out), F32),
        compiler_params=_params("parallel", "parallel"),
        name="ada",
    )(cc, w_ada, b_ada.reshape(depth, 1, n_out))


def _s5_prep_body(are_ref, aim_ref, ldt_ref, are_b_ref, aim_b_ref, ldt_b_ref, bre_ref, bim_ref,
                  abr_ref, abi_ref, bbr_ref, bbi_ref):
    def zoh(a_re, a_im, log_dt):
        lr = jnp.minimum(a_re, -S5_MIN_DECAY)
        dt = jnp.exp(log_dt)
        mag = jnp.exp(lr * dt)
        ab_r = mag * jnp.cos(a_im * dt)
        ab_i = mag * jnp.sin(a_im * dt)
        return lr, ab_r, ab_i

    _, ab_r, ab_i = zoh(are_ref[...], aim_ref[...], ldt_ref[...])
    abr_ref[...] = ab_r
    abi_ref[...] = ab_i
    lr, ab_r, ab_i = zoh(are_b_ref[...], aim_b_ref[...], ldt_b_ref[...])
    li = aim_b_ref[...]
    nr, ni = ab_r - 1.0, ab_i
    den = lr * lr + li * li
    kr = (nr * lr + ni * li) / den
    ki = (ni * lr - nr * li) / den
    br, bi = bre_ref[...], bim_ref[...]
    bbr_ref[...] = kr * br - ki * bi
    bbi_ref[...] = kr * bi + ki * br


def _s5_prep(a_re, a_im, log_dt, b_re, b_im):
    lead = a_re.shape[:3]
    r = lead[0] * lead[1] * lead[2]
    p, h = S5_STATE, S5_GROUP_CH
    a2 = lambda t: t.reshape(r, p)
    ab = lambda t: jnp.broadcast_to(t.reshape(r, p, 1), (r, p, h)).reshape(r, p * h)
    ldt = jnp.broadcast_to(log_dt.reshape(r, 1), (r, p))
    ldt_b = jnp.broadcast_to(log_dt.reshape(r, 1), (r, p * h))
    small = jax.ShapeDtypeStruct((r, p), F32)
    big = jax.ShapeDtypeStruct((r, p * h), F32)
    abr, abi, bbr, bbi = pl.pallas_call(
        _s5_prep_body, out_shape=(small, small, big, big), name="s5_prep",
    )(a2(a_re), a2(a_im), ldt, ab(a_re), ab(a_im), ldt_b, b_re.reshape(r, p * h), b_im.reshape(r, p * h))
    return (abr.reshape(*lead, p), abi.reshape(*lead, p),
            bbr.reshape(*lead, p, h), bbi.reshape(*lead, p, h))


def _in_proj_body(*refs, rope):
    if rope:
        (x_ref, sh_ref, sc_ref, g_ref, w_ref, gq_ref, gk_ref, hm_ref, cos_ref, sa_ref, sb_ref,
         qa_ref, ka_ref, va_ref, us_ref, nq_ref, nk_ref, nv_ref, fn_ref) = refs
    else:
        (x_ref, sh_ref, sc_ref, g_ref, w_ref, gq_ref, gk_ref, hm_ref,
         qa_ref, ka_ref, va_ref, us_ref, nq_ref, nk_ref, nv_ref, fn_ref) = refs
    h = _rms(x_ref[...], g_ref[...]) * (1.0 + sc_ref[...]) + sh_ref[...]
    z = jnp.dot(h.astype(BF16), w_ref[...], preferred_element_type=F32)
    hm = hm_ref[...]

    def head_norm(t, g, avg):
        ms = jnp.dot(t * t, avg, precision=HIGHEST, preferred_element_type=F32)
        return t * lax.rsqrt(ms + EPS) * g

    q = head_norm(z[:, OFF_ATT_Q:OFF_ATT_K], gq_ref[...], hm)
    kw = OFF_ATT_V - OFF_ATT_K
    k = head_norm(z[:, OFF_ATT_K:OFF_ATT_V], gk_ref[...], hm[:kw, :kw])
    if rope:
        def rot(t):
            w = t.shape[-1]
            half = HEAD_DIM // 4
            return (t * cos_ref[:, :w] + pltpu.roll(t, w - half, 1) * sa_ref[:, :w]
                    + pltpu.roll(t, half, 1) * sb_ref[:, :w])
        q, k = rot(q), rot(k)
    scale = HEAD_DIM ** -0.5
    qa_ref[...] = (q * scale).astype(qa_ref.dtype)
    ka_ref[...] = k.astype(ka_ref.dtype)
    va_ref[...] = z[:, OFF_ATT_V:OFF_S5].astype(va_ref.dtype)
    us_ref[...] = z[:, OFF_S5:OFF_NA_Q].astype(us_ref.dtype)
    nq_ref[...] = (z[:, OFF_NA_Q:OFF_NA_K] * scale).astype(nq_ref.dtype)
    nk_ref[...] = z[:, OFF_NA_K:OFF_NA_V].astype(nk_ref.dtype)
    nv_ref[...] = z[:, OFF_NA_V:OFF_FN].astype(nv_ref.dtype)
    fn_ref[...] = z[:, OFF_FN:IN_WIDTH].astype(fn_ref.dtype)


def _in_proj(x, sh, sc, g_pre, w_in, gq, gk, hm, rope_tabs, tm):
    b, n, d = x.shape
    rope = rope_tabs is not None
    row = lambda w: pl.BlockSpec((None, tm, w), lambda bi, i: (bi, i, 0))
    mod = pl.BlockSpec((None, 1, d), lambda bi, i: (bi, 0, 0))
    in_specs = [row(d), mod, mod, _const_spec((1, d)), _const_spec(w_in.shape),
                _const_spec(gq.shape), _const_spec(gk.shape), _const_spec(hm.shape)]
    args = [x, sh, sc, g_pre, w_in, gq, gk, hm]
    if rope:
        in_specs += [pl.BlockSpec((tm, GW), lambda bi, i: (i, 0))] * 3
        args += list(rope_tabs)
    kvw = ATT_KV_HEADS * HEAD_DIM
    widths = [GW, kvw, kvw, GW, GW, GW, GW, GW]
    out_specs = [row(w) for w in widths]
    out_shape = [jax.ShapeDtypeStruct((b, n, w), BF16) for w in widths]
    return pl.pallas_call(
        functools.partial(_in_proj_body, rope=rope),
        grid=(b, n // tm), in_specs=in_specs, out_specs=out_specs, out_shape=out_shape,
        compiler_params=_params("parallel", "parallel"), name="in_proj_rope" if rope else "in_proj",
    )(*args)


def _mha_body(*refs, n_heads, group, two_sets):
    if two_sets:
        q_ref, k1_ref, v1_ref, k2_ref, v2_ref, o_ref = refs
    else:
        q_ref, k1_ref, v1_ref, o_ref = refs
    for h in range(n_heads):
        qs = slice(HEAD_DIM * h, HEAD_DIM * (h + 1))
        ks = slice(HEAD_DIM * (h // group), HEAD_DIM * (h // group + 1))
        qh = q_ref[:, qs]
        s1 = _nt_dot(qh, k1_ref[:, ks])
        m = jnp.max(s1, axis=-1, keepdims=True)
        if two_sets:
            s2 = _nt_dot(qh, k2_ref[:, ks])
            m = jnp.maximum(m, jnp.max(s2, axis=-1, keepdims=True))
        p1 = jnp.exp(s1 - m)
        l = jnp.sum(p1, axis=-1, keepdims=True)
        o = jnp.dot(p1.astype(BF16), v1_ref[:, ks], preferred_element_type=F32)
        if two_sets:
            p2 = jnp.exp(s2 - m)
            l = l + jnp.sum(p2, axis=-1, keepdims=True)
            o = o + jnp.dot(p2.astype(BF16), v2_ref[:, ks], preferred_element_type=F32)
        o_ref[:, qs] = (o / l).astype(o_ref.dtype)


def _mha(q, k1, v1, k2=None, v2=None, *, group, tq):
    b, nq, qw = q.shape
    two_sets = k2 is not None
    full = lambda t: pl.BlockSpec((None,) + t.shape[1:], lambda bi, i: (bi, 0, 0))
    args = [q, k1, v1] + ([k2, v2] if two_sets else [])
    in_specs = [pl.BlockSpec((None, tq, qw), lambda bi, i: (bi, i, 0))] + [full(t) for t in args[1:]]
    return pl.pallas_call(
        functools.partial(_mha_body, n_heads=qw // HEAD_DIM, group=group, two_sets=two_sets),
        grid=(b, nq // tq), in_specs=in_specs,
        out_specs=pl.BlockSpec((None, tq, qw), lambda bi, i: (bi, i, 0)),
        out_shape=jax.ShapeDtypeStruct((b, nq, qw), BF16),
        compiler_params=_params("parallel", "parallel"), name="mha2" if two_sets else "mha1",
    )(*args)


def _na_body(q_ref, k_ref, v_ref, kc_ref, vc_ref, bias_ref, o_ref, *, rt, k_r, rows):
    i = pl.program_id(1)
    kc = kc_ref[...]
    vc = vc_ref[...]
    lane_head = lax.broadcasted_iota(jnp.int32, (GRID_W, GW), 1) // HEAD_DIM
    for j in range(rt):
        r = i * rt + j
        rs = jnp.clip(r - k_r // 2, 0, rows - k_r)
        start = pl.multiple_of(rs * GRID_W, GRID_W)
        kw = k_ref[pl.ds(start, k_r * GRID_W), :]
        vw = v_ref[pl.ds(start, k_r * GRID_W), :]
        q = q_ref[j * GRID_W:(j + 1) * GRID_W, :]
        q4 = jnp.concatenate([jnp.where(lane_head == h, q, jnp.zeros_like(q)) for h in range(NA_HEADS)], axis=0)
        s_loc = _nt_dot(q4, kw) + bias_ref[r - rs]
        s_ctx = _nt_dot(q4, kc)
        m = jnp.maximum(jnp.max(s_loc, axis=-1, keepdims=True), jnp.max(s_ctx, axis=-1, keepdims=True))
        p_loc = jnp.exp(s_loc - m)
        p_ctx = jnp.exp(s_ctx - m)
        l = jnp.sum(p_loc, axis=-1, keepdims=True) + jnp.sum(p_ctx, axis=-1, keepdims=True)
        o4 = (jnp.dot(p_loc.astype(BF16), vw, preferred_element_type=F32)
              + jnp.dot(p_ctx.astype(BF16), vc, preferred_element_type=F32)) / l
        o = jnp.zeros((GRID_W, GW), F32)
        for h in range(NA_HEADS):
            o = o + jnp.where(lane_head == h, o4[h * GRID_W:(h + 1) * GRID_W, :], 0.0)
        o_ref[j * GRID_W:(j + 1) * GRID_W, :] = o.astype(o_ref.dtype)


def _na(q, k, v, kc, vc, bias, rt):
    b, n, _ = q.shape
    rows = n // GRID_W
    k_r = bias.shape[0]
    full = lambda t: pl.BlockSpec((None,) + t.shape[1:], lambda bi, i: (bi, 0, 0))
    return pl.pallas_call(
        functools.partial(_na_body, rt=rt, k_r=k_r, rows=rows),
        grid=(b, rows // rt),
        in_specs=[pl.BlockSpec((None, rt * GRID_W, GW), lambda bi, i: (bi, i, 0)),
                  full(k), full(v), full(kc), full(vc), _const_spec(bias.shape)],
        out_specs=pl.BlockSpec((None, rt * GRID_W, GW), lambda bi, i: (bi, i, 0)),
        out_shape=jax.ShapeDtypeStruct((b, n, GW), BF16),
        compiler_params=_params("parallel", "parallel"), name="na",
    )(q, k, v, kc, vc, bias)


def _na_bias_table(rel_bias, rows):
    k_r = min(NA_ROWS, rows)
    cols = np.arange(GRID_W)
    col_start = np.clip(cols - NA_COLS // 2, 0, GRID_W - NA_COLS)
    inside = (cols[None, :] >= col_start[:, None]) & (cols[None, :] < col_start[:, None] + NA_COLS)
    rel_c = cols[None, :] - cols[:, None] + (NA_COLS - 1)
    rel_r = np.arange(k_r)[None, :] - np.arange(k_r)[:, None] + (NA_ROWS - 1)
    pick_r = (rel_r[:, :, None] == np.arange(2 * NA_ROWS - 1)).astype(np.float32)
    pick_c = ((rel_c[:, :, None] == np.arange(2 * NA_COLS - 1)) & inside[:, :, None]).astype(np.float32)
    t = jnp.einsum("cai,him->cham", pick_r, rel_bias.astype(F32), precision=HIGHEST)
    t = jnp.einsum("cham,jkm->chjak", t, pick_c, precision=HIGHEST)
    t = t + np.where(inside, 0.0, MASK_VALUE).astype(np.float32)[None, None, :, None, :]
    return t.reshape(k_r, NA_HEADS * GRID_W, k_r * GRID_W)


def _s5_scan(hs_ref, st_ref, a_ref, *, tc, nb, n_state, col_w, reverse):
    for c0 in range(0, n_state, col_w):
        re = slice(c0, c0 + col_w)
        im = slice(n_state + c0, n_state + c0 + col_w)
        ar = jnp.broadcast_to(a_ref[:, re], (nb, col_w))
        ai = jnp.broadcast_to(a_ref[:, im], (nb, col_w))

        def step(t, carry):
            hr, hi = carry
            rows = pl.ds(pl.multiple_of((tc - 1 - t if reverse else t) * nb, nb), nb)
            nr = ar * hr - ai * hi + hs_ref[rows, re]
            ni = ar * hi + ai * hr + hs_ref[rows, im]
            hs_ref[rows, re] = nr
            hs_ref[rows, im] = ni
            return nr, ni

        hr, hi = lax.fori_loop(0, tc, step, (st_ref[:, re], st_ref[:, im]), unroll=2)
        st_ref[:, re] = hr
        st_ref[:, im] = hi


def _s5_fwd_body(uc_ref, ul_ref, perm_ref, bd_ref, cd_ref, a_ref, ut_ref, yf_ref, hs_ref, st_ref, *, n_c, scan):
    j = pl.program_id(0)

    @pl.when(j == 0)
    def _():
        st_ref[...] = jnp.zeros_like(st_ref)

    def to_time_major(src_ref):
        nb, tc, w = src_ref.shape
        ut_ref[...] = jnp.dot(perm_ref[...], src_ref[...].reshape(nb * tc, w),
                              preferred_element_type=F32).astype(ut_ref.dtype)

    pl.when(j < n_c)(lambda: to_time_major(uc_ref))
    pl.when(j >= n_c)(lambda: to_time_major(ul_ref))
    hs_ref[...] = jnp.dot(ut_ref[...], bd_ref[...], preferred_element_type=F32)
    scan(hs_ref, st_ref, a_ref, reverse=False)
    yf_ref[...] = jnp.dot(hs_ref[...].astype(BF16), cd_ref[...], preferred_element_type=F32)


def _s5_bwd_body(ut_ref, yf_ref, permt_ref, bd_ref, cd_ref, a_ref, dsk_ref, wglu_ref, oc_ref, ol_ref,
                 hs_ref, st_ref, *, n_c, scan):
    j = pl.program_id(0)

    @pl.when(j == 0)
    def _():
        st_ref[...] = jnp.zeros_like(st_ref)

    u = ut_ref[...]
    hs_ref[...] = jnp.dot(u, bd_ref[...], preferred_element_type=F32)
    scan(hs_ref, st_ref, a_ref, reverse=True)
    t = (dsk_ref[...] * u.astype(F32) + yf_ref[...]
         + jnp.dot(hs_ref[...].astype(BF16), cd_ref[...], preferred_element_type=F32))
    g = jax.nn.gelu(t)
    ob = g * jax.nn.sigmoid(jnp.dot(g.astype(BF16), wglu_ref[...], preferred_element_type=F32))
    ob = jnp.dot(permt_ref[...], ob.astype(BF16), preferred_element_type=F32).astype(BF16)

    @pl.when(j < n_c)
    def _():
        oc_ref[...] = ob.reshape(oc_ref.shape)

    @pl.when(j >= n_c)
    def _():
        ol_ref[...] = ob.reshape(ol_ref.shape)


def _s5(usc, us, bd, cd, a_row, dsk, wglu, tc):
    nb, lc, _ = usc.shape
    n = us.shape[1]
    n_c, n_l = lc // tc, n // tc
    n_all = n_c + n_l
    n_state2 = bd.shape[-1]
    rows = tc * nb
    r = np.arange(rows)
    perm = np.zeros((rows, rows), np.float32)
    perm[r, (r % nb) * tc + r // nb] = 1.0
    scan = functools.partial(_s5_scan, tc=tc, nb=nb, n_state=n_state2 // 2, col_w=512)
    scratch = [pltpu.VMEM((rows, n_state2), F32), pltpu.VMEM((nb, n_state2), F32)]
    seg = lambda idx: pl.BlockSpec((nb, tc, GW), lambda j: (0, idx(j), 0))
    tm_rows = lambda idx: pl.BlockSpec((rows, GW), lambda j: (idx(j), 0))

    ut, yf = pl.pallas_call(
        functools.partial(_s5_fwd_body, n_c=n_c, scan=scan), grid=(n_all,),
        in_specs=[seg(lambda j: jnp.minimum(j, n_c - 1)), seg(lambda j: jnp.maximum(j - n_c, 0)),
                  _const_spec(perm.shape), _const_spec(bd.shape[1:]), _const_spec(cd.shape[1:]),
                  _const_spec(a_row.shape[1:])],
        out_specs=[tm_rows(lambda j: j), tm_rows(lambda j: j)],
        out_shape=[jax.ShapeDtypeStruct((n_all * rows, GW), BF16), jax.ShapeDtypeStruct((n_all * rows, GW), F32)],
        scratch_shapes=scratch, compiler_params=_params("arbitrary"), name="s5_fwd",
    )(usc, us, jnp.asarray(perm, BF16), bd[0], cd[0], a_row[0])

    blk = lambda j: jnp.where(j < n_c, n_c - 1 - j, n_c + n_all - 1 - j)
    return pl.pallas_call(
        functools.partial(_s5_bwd_body, n_c=n_c, scan=scan), grid=(n_all,),
        in_specs=[tm_rows(blk), tm_rows(blk), _const_spec(perm.shape), _const_spec(bd.shape[1:]),
                  _const_spec(cd.shape[1:]), _const_spec(a_row.shape[1:]), _const_spec(dsk.shape),
                  _const_spec(wglu.shape)],
        out_specs=[seg(lambda j: jnp.maximum(n_c - 1 - j, 0)), seg(lambda j: n_l - 1 - jnp.maximum(j - n_c, 0))],
        out_shape=[jax.ShapeDtypeStruct(usc.shape, BF16), jax.ShapeDtypeStruct(us.shape, BF16)],
        scratch_shapes=scratch, compiler_params=_params("arbitrary"), name="s5_bwd",
    )(ut, yf, jnp.asarray(perm.T, BF16), bd[1], cd[1], a_row[1], dsk, wglu)


def _s5_matrices(abr, abi, bbr, bbi, c_re, c_im):
    g, p, h = S5_GROUPS, S5_STATE, S5_GROUP_CH
    eye = jnp.eye(g, dtype=F32)
    to_bd = lambda t: jnp.einsum("dgph,gk->dghkp", t, eye).reshape(2, g * h, g * p)
    bd = jnp.concatenate([to_bd(bbr), to_bd(bbi)], axis=-1)
    to_cd = lambda t: jnp.einsum("dghp,gk->dgpkh", t, eye).reshape(2, g * p, g * h)
    cd = jnp.concatenate([to_cd(c_re.astype(F32)), to_cd(-c_im.astype(F32))], axis=1)
    a_row = jnp.concatenate([abr.reshape(2, 1, g * p), abi.reshape(2, 1, g * p)], axis=-1)
    return bd.astype(BF16), cd.astype(BF16), a_row


def _fnet_body(u_ref, dft_ref, ccs_ref, w_ref, b_ref, o_ref, uc_ref):
    n = u_ref.shape[0]

    @pl.when(pl.program_id(1) == 0)
    def _():
        t = jnp.dot(u_ref[...], ccs_ref[...], preferred_element_type=F32)
        uc_ref[0:n, :] = t[:, :GW].astype(BF16)
        uc_ref[n:2 * n, :] = t[:, GW:].astype(BF16)

    f = jnp.dot(dft_ref[...], uc_ref[...], preferred_element_type=F32)
    o = jnp.dot(f.astype(BF16), w_ref[...], preferred_element_type=F32) + b_ref[...]
    o_ref[...] = o.astype(o_ref.dtype)


def _fnet(u, dft, ccs, w, bias, tk):
    b, n, _ = u.shape
    return pl.pallas_call(
        _fnet_body, grid=(b, n // tk),
        in_specs=[pl.BlockSpec((None, n, GW), lambda bi, k: (bi, 0, 0)),
                  pl.BlockSpec((tk, 2 * n), lambda bi, k: (k, 0)),
                  _const_spec(ccs.shape), _const_spec(w.shape), _const_spec(bias.shape)],
        out_specs=pl.BlockSpec((None, tk, GW), lambda bi, k: (bi, k, 0)),
        out_shape=jax.ShapeDtypeStruct((b, n, GW), BF16),
        scratch_shapes=[pltpu.VMEM((2 * n, GW), BF16)],
        compiler_params=_params("parallel", "arbitrary"), name="fnet",
    )(u, dft, ccs, w, bias)


def _dft_matrices(n):
    ch = GW // FN_GROUPS
    k = jnp.arange(n)
    ang = (2.0 * math.pi / n) * ((k[:, None] * k[None, :]) % n).astype(F32)
    dft = jnp.concatenate([jnp.cos(ang), -jnp.sin(ang)], axis=1).astype(BF16)
    c = jnp.arange(GW)
    same = (c[:, None] // ch) == (c[None, :] // ch)
    angc = (2.0 * math.pi / ch) * (((c[:, None] % ch) * (c[None, :] % ch)) % ch).astype(F32)
    norm = 1.0 / math.sqrt(n * ch)
    cc = jnp.where(same, jnp.cos(angc), 0.0) * norm
    sc = jnp.where(same, jnp.sin(angc), 0.0) * norm
    return dft, jnp.concatenate([cc, sc], axis=1).astype(BF16)


def _out_mlp_body(x_ref, oa_ref, ob_ref, on_ref, od_ref, g1_ref, sh2_ref, sc2_ref, g2_ref,
                  wout_ref, gpm_ref, gpre_ref, gpost_ref, w1_ref, w2_ref, o_ref, *, ff_chunk):
    cat = jnp.concatenate([oa_ref[...], ob_ref[...], on_ref[...], od_ref[...]], axis=-1)
    y = jnp.dot(cat, wout_ref[...], preferred_element_type=F32)
    x1 = x_ref[...] + g1_ref[...] * _rms(y, gpm_ref[...])
    h = (_rms(x1, gpre_ref[...]) * (1.0 + sc2_ref[...]) + sh2_ref[...]).astype(BF16)
    d_ff = w1_ref.shape[1]
    m = jnp.zeros(x1.shape, F32)
    for c0 in range(0, d_ff, ff_chunk):
        a = jnp.maximum(jnp.dot(h, w1_ref[:, c0:c0 + ff_chunk], preferred_element_type=F32), 0.0)
        m = m + jnp.dot((a * a).astype(BF16), w2_ref[c0:c0 + ff_chunk, :], preferred_element_type=F32)
    o_ref[...] = x1 + g2_ref[...] * _rms(m, gpost_ref[...])


def _out_mlp(x, oa, ob, on, od, mods, wout, gpm, gpre, gpost, w1, w2, tm):
    b, n, d = x.shape
    row = lambda w: pl.BlockSpec((None, tm, w), lambda bi, i: (bi, i, 0))
    mod = pl.BlockSpec((None, 1, d), lambda bi, i: (bi, 0, 0))
    consts = [wout, gpm, gpre, gpost, w1, w2]
    return pl.pallas_call(
        functools.partial(_out_mlp_body, ff_chunk=1024),
        grid=(b, n // tm),
        in_specs=[row(d), row(GW), row(GW), row(GW), row(GW), mod, mod, mod, mod]
        + [_const_spec(t.shape) for t in consts],
        out_specs=row(d), out_shape=jax.ShapeDtypeStruct((b, n, d), F32),
        compiler_params=_params("parallel", "parallel"), name="out_mlp",
    )(x, oa, ob, on, od, *mods, *consts)


def _rope_tables(n):
    half = HEAD_DIM // 4
    pos = jnp.arange(n, dtype=jnp.int32)
    row = (pos // GRID_W).astype(F32)
    col = (pos % GRID_W).astype(F32)
    lane = jnp.arange(GW)
    jj = lane % HEAD_DIM
    inv = ROPE_BASE ** (-(jj % half).astype(F32) / half)
    p = jnp.where((jj // (2 * half) == 0)[None, :], row[:, None], col[:, None])
    ang = p * inv[None, :]
    first = ((jj % (2 * half)) < half)[None, :]
    sin = jnp.sin(ang)
    return jnp.cos(ang), jnp.where(first, -sin, 0.0), jnp.where(first, 0.0, sin)


def kernel(x, c, ctx, c_ctx, w_ada, b_ada, g_pre_mix, g_post_mix, g_pre_mlp, g_post_mlp, w_in, g_q_attn, g_k_attn,
           s5_a_re, s5_a_im, s5_log_dt, s5_b_re, s5_b_im, s5_c_re, s5_c_im, s5_d, w_s5_glu, na_rel_bias,
           w_fnet, b_fnet, w_out, w_mlp1, w_mlp2):
    b, n, d = x.shape
    lc = ctx.shape[1]
    depth = w_ada.shape[0]
    rows = n // GRID_W

    n_rows = -(-(b + 1) // 8) * 8
    cc = jnp.concatenate([c, c_ctx[None, :], jnp.zeros((n_rows - b - 1, d), c.dtype)], axis=0)
    mod_all = _ada(cc, w_ada, b_ada).reshape(depth, n_rows, N_MOD, d)

    abr, abi, bbr, bbi = _s5_prep(s5_a_re, s5_a_im, s5_log_dt, s5_b_re, s5_b_im)
    rope_tabs = _rope_tables(n)
    lane = jnp.arange(GW)
    hm = jnp.where((lane[:, None] // HEAD_DIM) == (lane[None, :] // HEAD_DIM), 1.0 / HEAD_DIM, 0.0).astype(F32)
    dft_n, ccs_n = _dft_matrices(n)
    dft_c, ccs_c = _dft_matrices(lc)

    tm_in = min(512, n)
    tm_c = min(256, lc)
    tc = math.gcd(64, math.gcd(n, lc))

    xc = ctx
    for l in range(depth):
        need_ctx = l < depth - 1
        mods = [mod_all[l, :b, i][:, None, :] for i in range(N_MOD)]
        mods_c = [jnp.broadcast_to(mod_all[l, b, i][None, None, :], (b, 1, d)) for i in range(N_MOD)]
        vec = lambda t: t[l].reshape(1, -1).astype(F32)
        w_in_l = w_in[l].astype(BF16)
        gq = jnp.tile(g_q_attn[l].astype(F32), ATT_HEADS)[None, :]
        gk = jnp.tile(g_k_attn[l].astype(F32), ATT_KV_HEADS)[None, :]

        qa, ka, va, us, nq, nk, nv, fn = _in_proj(x, mods[0], mods[1], vec(g_pre_mix), w_in_l, gq, gk, hm,
                                                  rope_tabs, tm_in)
        qac, kac, vac, usc, nqc, nkc, nvc, fnc = _in_proj(xc, mods_c[0], mods_c[1], vec(g_pre_mix), w_in_l,
                                                          gq, gk, hm, None, tm_c)

        oa = _mha(qa, ka, va, kac, vac, group=ATT_HEADS // ATT_KV_HEADS, tq=min(256, n))
        on = _na(nq, nk, nv, nkc, nvc, _na_bias_table(na_rel_bias[l], rows), rt=min(4, rows))
        w_fn = w_fnet[l].astype(BF16)
        od = _fnet(fn, dft_n, ccs_n, w_fn, vec(b_fnet), tk=min(512, n))

        bd, cd, a_row = _s5_matrices(abr[l], abi[l], bbr[l], bbi[l], s5_c_re[l], s5_c_im[l])
        obc, ob = _s5(usc, us, bd, cd, a_row, vec(s5_d), w_s5_glu[l].astype(BF16), tc)

        consts = (w_out[l].astype(BF16), vec(g_post_mix), vec(g_pre_mlp), vec(g_post_mlp),
                  w_mlp1[l].astype(BF16), w_mlp2[l].astype(BF16))
        x_new = _out_mlp(x, oa, ob, on, od, (mods[2], mods[3], mods[4], mods[5]), *consts, tm_in)
        if need_ctx:
            oac = _mha(qac, kac, vac, group=ATT_HEADS // ATT_KV_HEADS, tq=tm_c)
            onc = _mha(nqc, nkc, nvc, group=1, tq=tm_c)
            odc = _fnet(fnc, dft_c, ccs_c, w_fn, vec(b_fnet), tk=tm_c)
            xc = _out_mlp(xc, oac, obc, onc, odc, (mods_c[2], mods_c[3], mods_c[4], mods_c[5]), *consts, tm_c)
        x = x_new
    return x
```

```python
import functools
import math

import jax
import jax.numpy as jnp
import numpy as np
from jax import lax
from jax.experimental import pallas as pl
from jax.experimental.pallas import tpu as pltpu

F32 = jnp.float32
BF16 = jnp.bfloat16
HIGHEST = lax.Precision.HIGHEST

HEAD_DIM = 64
GRID_W = 64
N_MOD = 6
EPS = 1e-6
ATT_HEADS = 4
ATT_KV_HEADS = 2
ROPE_BASE = 10000.0
S5_GROUPS = 16
S5_GROUP_CH = 16
S5_STATE = 64
S5_MIN_DECAY = 1e-4
NA_HEADS = 4
NA_ROWS = 8
NA_COLS = 16
FN_GROUPS = 4
GW = 256
MASK_VALUE = -1e30
LOG2E = 1.4426950408889634

OFF_ATT_Q, OFF_ATT_K, OFF_ATT_V, OFF_S5 = 0, 256, 384, 512
OFF_NA_Q, OFF_NA_K, OFF_NA_V, OFF_FN, IN_WIDTH = 768, 1024, 1280, 1536, 1792

VMEM_LIMIT = 56 * 1024 * 1024


def _params(*sem):
    return pltpu.CompilerParams(dimension_semantics=sem, vmem_limit_bytes=VMEM_LIMIT)


def _const_spec(shape):
    return pl.BlockSpec(shape, lambda *_: (0,) * len(shape), pipeline_mode=pl.Buffered(1))


def _rms(x, g):
    return x * lax.rsqrt(jnp.mean(x * x, axis=-1, keepdims=True) + EPS) * g


def _dot2(a, b):
    half = a.shape[0] // 2
    return jnp.concatenate([jnp.dot(a[:half], b, preferred_element_type=F32),
                            jnp.dot(a[half:], b, preferred_element_type=F32)], axis=0)


def _nt_dot(a, b):
    return lax.dot_general(a, b, (((1,), (1,)), ((), ())), preferred_element_type=F32)


def _ada_body(c_ref, w_ref, b_ref, o_ref):
    c = c_ref[...]
    s = c * jax.nn.sigmoid(c)
    o_ref[...] = jnp.dot(s, w_ref[...], precision=HIGHEST, preferred_element_type=F32) + b_ref[...]


def _ada(cc, w_ada, b_ada, tn=1536):
    depth, d, n_out = w_ada.shape
    rows = cc.shape[0]
    return pl.pallas_call(
        _ada_body,
        grid=(depth, n_out // tn),
        in_specs=[
            pl.BlockSpec((rows, d), lambda l, j: (0, 0)),
            pl.BlockSpec((None, d, tn), lambda l, j: (l, 0, j)),
            pl.BlockSpec((None, 1, tn), lambda l, j: (l, 0, j)),
        ],
        out_specs=pl.BlockSpec((None, rows, tn), lambda l, j: (l, 0, j)),
        out_shape=jax.ShapeDtypeStruct((depth, rows, n_out), F32),
        compiler_params=_params("parallel", "parallel"),
        name="ada",
    )(cc, w_ada, b_ada.reshape(depth, 1, n_out))


def _s5_prep_body(are_ref, aim_ref, ldt_ref, are_b_ref, aim_b_ref, ldt_b_ref, bre_ref, bim_ref,
                  abr_ref, abi_ref, bbr_ref, bbi_ref):
    def zoh(a_re, a_im, log_dt):
        lr = jnp.minimum(a_re, -S5_MIN_DECAY)
        dt = jnp.exp(log_dt)
        mag = jnp.exp(lr * dt)
        ab_r = mag * jnp.cos(a_im * dt)
        ab_i = mag * jnp.sin(a_im * dt)
        return lr, ab_r, ab_i

    _, ab_r, ab_i = zoh(are_ref[...], aim_ref[...], ldt_ref[...])
    abr_ref[...] = ab_r
    abi_ref[...] = ab_i
    lr, ab_r, ab_i = zoh(are_b_ref[...], aim_b_ref[...], ldt_b_ref[...])
    li = aim_b_ref[...]
    nr, ni = ab_r - 1.0, ab_i
    den = lr * lr + li * li
    kr = (nr * lr + ni * li) / den
    ki = (ni * lr - nr * li) / den
    br, bi = bre_ref[...], bim_ref[...]
    bbr_ref[...] = kr * br - ki * bi
    bbi_ref[...] = kr * bi + ki * br


def _s5_prep(a_re, a_im, log_dt, b_re, b_im):
    lead = a_re.shape[:3]
    r = lead[0] * lead[1] * lead[2]
    p, h = S5_STATE, S5_GROUP_CH
    a2 = lambda t: t.reshape(r, p)
    ab = lambda t: jnp.broadcast_to(t.reshape(r, p, 1), (r, p, h)).reshape(r, p * h)
    ldt = jnp.broadcast_to(log_dt.reshape(r, 1), (r, p))
    ldt_b = jnp.broadcast_to(log_dt.reshape(r, 1), (r, p * h))
    small = jax.ShapeDtypeStruct((r, p), F32)
    big = jax.ShapeDtypeStruct((r, p * h), F32)
    abr, abi, bbr, bbi = pl.pallas_call(
        _s5_prep_body, out_shape=(small, small, big, big), name="s5_prep",
    )(a2(a_re), a2(a_im), ldt, ab(a_re), ab(a_im), ldt_b, b_re.reshape(r, p * h), b_im.reshape(r, p * h))
    return (abr.reshape(*lead, p), abi.reshape(*lead, p),
            bbr.reshape(*lead, p, h), bbi.reshape(*lead, p, h))


def _in_proj_body(*refs, rope):
    if rope:
        (x_ref, sh_ref, sc_ref, g_ref, w_ref, gq_ref, gk_ref, hm_ref, cos_ref, sa_ref, sb_ref,
         qa_ref, ka_ref, va_ref, us_ref, nq_ref, nk_ref, nv_ref, fn_ref) = refs
    else:
        (x_ref, sh_ref, sc_ref, g_ref, w_ref, gq_ref, gk_ref, hm_ref,
         qa_ref, ka_ref, va_ref, us_ref, nq_ref, nk_ref, nv_ref, fn_ref) = refs
    h = _rms(x_ref[...], g_ref[...]) * (1.0 + sc_ref[...]) + sh_ref[...]
    hb = h.astype(BF16)
    z_lo = jnp.dot(hb, w_ref[:, :OFF_NA_K], preferred_element_type=F32)
    z_hi = jnp.dot(hb, w_ref[:, OFF_NA_K:], preferred_element_type=F32)
    hm = hm_ref[...]

    def head_norm(t, g, avg):
        ms = jnp.dot((t * t).astype(BF16), avg, preferred_element_type=F32)
        return t * lax.rsqrt(ms + EPS) * g

    q = head_norm(z_lo[:, OFF_ATT_Q:OFF_ATT_K], gq_ref[...], hm)
    kw = OFF_ATT_V - OFF_ATT_K
    k = head_norm(z_lo[:, OFF_ATT_K:OFF_ATT_V], gk_ref[...], hm[:kw, :kw])
    if rope:
        def rot(t):
            w = t.shape[-1]
            half = HEAD_DIM // 4
            return (t * cos_ref[:, :w] + pltpu.roll(t, w - half, 1) * sa_ref[:, :w]
                    + pltpu.roll(t, half, 1) * sb_ref[:, :w])
        q, k = rot(q), rot(k)
    scale = HEAD_DIM ** -0.5 * LOG2E
    qa_ref[...] = (q * scale).astype(qa_ref.dtype)
    ka_ref[...] = k.astype(ka_ref.dtype)
    v = z_lo[:, OFF_ATT_V:OFF_S5].astype(va_ref.dtype)
    lane = lax.broadcasted_iota(jnp.int32, (v.shape[0], HEAD_DIM), 1)
    one_col = jnp.where(lane == 0, 1.0, 0.0).astype(va_ref.dtype)
    va_ref[...] = jnp.concatenate(
        sum(([v[:, HEAD_DIM * i:HEAD_DIM * (i + 1)], one_col] for i in range(ATT_KV_HEADS)), []), axis=-1)
    us_ref[...] = z_lo[:, OFF_S5:OFF_NA_Q].astype(us_ref.dtype)
    nq_ref[...] = (z_lo[:, OFF_NA_Q:OFF_NA_K] * scale).astype(nq_ref.dtype)
    nk_ref[...] = z_hi[:, :GW].astype(nk_ref.dtype)
    nv_ref[...] = z_hi[:, GW:2 * GW].astype(nv_ref.dtype)
    fn_ref[...] = z_hi[:, 2 * GW:].astype(fn_ref.dtype)


def _in_proj(x, sh, sc, g_pre, w_in, gq, gk, hm, rope_tabs, tm):
    b, n, d = x.shape
    rope = rope_tabs is not None
    row = lambda w: pl.BlockSpec((None, tm, w), lambda bi, i: (bi, i, 0))
    mod = pl.BlockSpec((None, 1, d), lambda bi, i: (bi, 0, 0))
    in_specs = [row(d), mod, mod, _const_spec((1, d)), _const_spec(w_in.shape),
                _const_spec(gq.shape), _const_spec(gk.shape), _const_spec(hm.shape)]
    args = [x, sh, sc, g_pre, w_in, gq, gk, hm]
    if rope:
        in_specs += [pl.BlockSpec((tm, GW), lambda bi, i: (i, 0))] * 3
        args += list(rope_tabs)
    kvw = ATT_KV_HEADS * HEAD_DIM
    widths = [GW, kvw, 2 * kvw, GW, GW, GW, GW, GW]
    out_specs = [row(w) for w in widths]
    out_shape = [jax.ShapeDtypeStruct((b, n, w), BF16) for w in widths]
    return pl.pallas_call(
        functools.partial(_in_proj_body, rope=rope),
        grid=(b, n // tm), in_specs=in_specs, out_specs=out_specs, out_shape=out_shape,
        compiler_params=_params("parallel", "parallel"), name="in_proj_rope" if rope else "in_proj",
    )(*args)


def _mha_body(*refs, n_heads, group, two_sets, v_ones):
    if two_sets:
        q_ref, k1_ref, v1_ref, k2_ref, v2_ref, o_ref = refs
    else:
        q_ref, k1_ref, v1_ref, o_ref = refs
    vw = 2 * HEAD_DIM if v_ones else HEAD_DIM
    for h in range(n_heads):
        qs = slice(HEAD_DIM * h, HEAD_DIM * (h + 1))
        ks = slice(HEAD_DIM * (h // group), HEAD_DIM * (h // group + 1))
        vs = slice(vw * (h // group), vw * (h // group + 1))
        qh = q_ref[:, qs]
        s1 = _nt_dot(qh, k1_ref[:, ks])
        m = jnp.max(s1, axis=-1, keepdims=True)
        if two_sets:
            s2 = _nt_dot(qh, k2_ref[:, ks])
            m = jnp.maximum(m, jnp.max(s2, axis=-1, keepdims=True))
        p1 = jnp.exp2(s1 - m)
        o = jnp.dot(p1.astype(BF16), v1_ref[:, vs], preferred_element_type=F32)
        if two_sets:
            p2 = jnp.exp2(s2 - m)
            o = o + jnp.dot(p2.astype(BF16), v2_ref[:, vs], preferred_element_type=F32)
        if v_ones:
            l = o[:, HEAD_DIM:HEAD_DIM + 1]
        else:
            l = jnp.sum(p1, axis=-1, keepdims=True)
            if two_sets:
                l = l + jnp.sum(p2, axis=-1, keepdims=True)
        o_ref[:, qs] = (o[:, :HEAD_DIM] / l).astype(o_ref.dtype)


def _mha(q, k1, v1, k2=None, v2=None, *, group, tq, v_ones):
    b, nq, qw = q.shape
    two_sets = k2 is not None
    full = lambda t: pl.BlockSpec((None,) + t.shape[1:], lambda bi, i: (bi, 0, 0))
    args = [q, k1, v1] + ([k2, v2] if two_sets else [])
    in_specs = [pl.BlockSpec((None, tq, qw), lambda bi, i: (bi, i, 0))] + [full(t) for t in args[1:]]
    return pl.pallas_call(
        functools.partial(_mha_body, n_heads=qw // HEAD_DIM, group=group, two_sets=two_sets, v_ones=v_ones),
        grid=(b, nq // tq), in_specs=in_specs,
        out_specs=pl.BlockSpec((None, tq, qw), lambda bi, i: (bi, i, 0)),
        out_shape=jax.ShapeDtypeStruct((b, nq, qw), BF16),
        compiler_params=_params("parallel", "parallel"), name="mha2" if two_sets else "mha1",
    )(*args)


def _na_body(q_ref, k_ref, v_ref, kc_ref, vc_ref, bias_ref, o_ref, *, rt, k_r, rows):
    i = pl.program_id(1)
    kc = kc_ref[...]
    vc = vc_ref[...]
    lane_head = lax.broadcasted_iota(jnp.int32, (GRID_W, GW), 1) // HEAD_DIM
    for j in range(rt):
        r = i * rt + j
        rs = jnp.clip(r - k_r // 2, 0, rows - k_r)
        start = pl.multiple_of(rs * GRID_W, GRID_W)
        kw = k_ref[pl.ds(start, k_r * GRID_W), :]
        vw = v_ref[pl.ds(start, k_r * GRID_W), :]
        q = q_ref[j * GRID_W:(j + 1) * GRID_W, :]
        q4 = jnp.concatenate([jnp.where(lane_head == h, q, jnp.zeros_like(q)) for h in range(NA_HEADS)], axis=0)
        s_loc = _nt_dot(q4, kw) + bias_ref[r - rs]
        s_ctx = _nt_dot(q4, kc)
        m = jnp.maximum(jnp.max(s_loc, axis=-1, keepdims=True), jnp.max(s_ctx, axis=-1, keepdims=True))
        p_loc = jnp.exp2(s_loc - m)
        p_ctx = jnp.exp2(s_ctx - m)
        l = jnp.sum(p_loc, axis=-1, keepdims=True) + jnp.sum(p_ctx, axis=-1, keepdims=True)
        o4 = (jnp.dot(p_loc.astype(BF16), vw, preferred_element_type=F32)
              + jnp.dot(p_ctx.astype(BF16), vc, preferred_element_type=F32)) / l
        o = jnp.zeros((GRID_W, GW), F32)
        for h in range(NA_HEADS):
            o = o + jnp.where(lane_head == h, o4[h * GRID_W:(h + 1) * GRID_W, :], 0.0)
        o_ref[j * GRID_W:(j + 1) * GRID_W, :] = o.astype(o_ref.dtype)


def _na(q, k, v, kc, vc, bias, rt):
    b, n, _ = q.shape
    rows = n // GRID_W
    k_r = bias.shape[0]
    full = lambda t: pl.BlockSpec((None,) + t.shape[1:], lambda bi, i: (bi, 0, 0))
    return pl.pallas_call(
        functools.partial(_na_body, rt=rt, k_r=k_r, rows=rows),
        grid=(b, rows // rt),
        in_specs=[pl.BlockSpec((None, rt * GRID_W, GW), lambda bi, i: (bi, i, 0)),
                  full(k), full(v), full(kc), full(vc), _const_spec(bias.shape)],
        out_specs=pl.BlockSpec((None, rt * GRID_W, GW), lambda bi, i: (bi, i, 0)),
        out_shape=jax.ShapeDtypeStruct((b, n, GW), BF16),
        compiler_params=_params("parallel", "parallel"), name="na",
    )(q, k, v, kc, vc, bias)


def _na_bias_table(rel_bias, rows):
    k_r = min(NA_ROWS, rows)
    cols = np.arange(GRID_W)
    col_start = np.clip(cols - NA_COLS // 2, 0, GRID_W - NA_COLS)
    inside = (cols[None, :] >= col_start[:, None]) & (cols[None, :] < col_start[:, None] + NA_COLS)
    rel_c = cols[None, :] - cols[:, None] + (NA_COLS - 1)
    rel_r = np.arange(k_r)[None, :] - np.arange(k_r)[:, None] + (NA_ROWS - 1)
    pick_r = (rel_r[:, :, None] == np.arange(2 * NA_ROWS - 1)).astype(np.float32)
    pick_c = ((rel_c[:, :, None] == np.arange(2 * NA_COLS - 1)) & inside[:, :, None]).astype(np.float32)
    t = jnp.einsum("cai,him->cham", pick_r, rel_bias.astype(F32), precision=HIGHEST)
    t = jnp.einsum("cham,jkm->chjak", t, pick_c, precision=HIGHEST)
    t = t + np.where(inside, 0.0, MASK_VALUE).astype(np.float32)[None, None, :, None, :]
    return (t * LOG2E).reshape(k_r, NA_HEADS * GRID_W, k_r * GRID_W)


def _s5_scan(hs_ref, st_ref, a_ref, *, tc, nb, n_state, col_w, reverse):
    for c0 in range(0, n_state, col_w):
        re = slice(c0, c0 + col_w)
        im = slice(n_state + c0, n_state + c0 + col_w)
        ar = jnp.broadcast_to(a_ref[:, re], (nb, col_w))
        ai = jnp.broadcast_to(a_ref[:, im], (nb, col_w))

        def step(t, carry):
            hr, hi = carry
            rows = pl.ds(pl.multiple_of((tc - 1 - t if reverse else t) * nb, nb), nb)
            nr = ar * hr - ai * hi + hs_ref[rows, re]
            ni = ar * hi + ai * hr + hs_ref[rows, im]
            hs_ref[rows, re] = nr
            hs_ref[rows, im] = ni
            return nr, ni

        hr, hi = lax.fori_loop(0, tc, step, (st_ref[:, re], st_ref[:, im]), unroll=2)
        st_ref[:, re] = hr
        st_ref[:, im] = hi


def _s5_fwd_body(uc_ref, ul_ref, perm_ref, bd_ref, cd_ref, a_ref, ut_ref, yf_ref, hs_ref, st_ref, *, n_c, scan):
    j = pl.program_id(0)

    @pl.when(j == 0)
    def _():
        st_ref[...] = jnp.zeros_like(st_ref)

    def to_time_major(src_ref):
        nb, tc, w = src_ref.shape
        ut_ref[...] = _dot2(perm_ref[...], src_ref[...].reshape(nb * tc, w)).astype(ut_ref.dtype)

    pl.when(j < n_c)(lambda: to_time_major(uc_ref))
    pl.when(j >= n_c)(lambda: to_time_major(ul_ref))
    hs_ref[...] = _dot2(ut_ref[...], bd_ref[...])
    scan(hs_ref, st_ref, a_ref, reverse=False)
    yf_ref[...] = _dot2(hs_ref[...].astype(BF16), cd_ref[...])


def _s5_bwd_body(ut_ref, yf_ref, permt_ref, bd_ref, cd_ref, a_ref, dsk_ref, wglu_ref, oc_ref, ol_ref,
                 hs_ref, st_ref, *, n_c, scan):
    j = pl.program_id(0)

    @pl.when(j == 0)
    def _():
        st_ref[...] = jnp.zeros_like(st_ref)

    u = ut_ref[...]
    hs_ref[...] = _dot2(u, bd_ref[...])
    scan(hs_ref, st_ref, a_ref, reverse=True)
    t = dsk_ref[...] * u.astype(F32) + yf_ref[...] + _dot2(hs_ref[...].astype(BF16), cd_ref[...])
    g = jax.nn.gelu(t)
    ob = g * jax.nn.sigmoid(_dot2(g.astype(BF16), wglu_ref[...]))
    ob = _dot2(permt_ref[...], ob.astype(BF16)).astype(BF16)

    @pl.when(j < n_c)
    def _():
        oc_ref[...] = ob.reshape(oc_ref.shape)

    @pl.when(j >= n_c)
    def _():
        ol_ref[...] = ob.reshape(ol_ref.shape)


def _s5(usc, us, bd, cd, a_row, dsk, wglu, tc):
    nb, lc, _ = usc.shape
    n = us.shape[1]
    n_c, n_l = lc // tc, n // tc
    n_all = n_c + n_l
    n_state2 = bd.shape[-1]
    rows = tc * nb
    r = np.arange(rows)
    perm = np.zeros((rows, rows), np.float32)
    perm[r, (r % nb) * tc + r // nb] = 1.0
    scan = functools.partial(_s5_scan, tc=tc, nb=nb, n_state=n_state2 // 2, col_w=512)
    scratch = [pltpu.VMEM((rows, n_state2), F32), pltpu.VMEM((nb, n_state2), F32)]
    seg = lambda idx: pl.BlockSpec((nb, tc, GW), lambda j: (0, idx(j), 0))
    tm_rows = lambda idx: pl.BlockSpec((rows, GW), lambda j: (idx(j), 0))

    ut, yf = pl.pallas_call(
        functools.partial(_s5_fwd_body, n_c=n_c, scan=scan), grid=(n_all,),
        in_specs=[seg(lambda j: jnp.minimum(j, n_c - 1)), seg(lambda j: jnp.maximum(j - n_c, 0)),
                  _const_spec(perm.shape), _const_spec(bd.shape[1:]), _const_spec(cd.shape[1:]),
                  _const_spec(a_row.shape[1:])],
        out_specs=[tm_rows(lambda j: j), tm_rows(lambda j: j)],
        out_shape=[jax.ShapeDtypeStruct((n_all * rows, GW), BF16), jax.ShapeDtypeStruct((n_all * rows, GW), F32)],
        scratch_shapes=scratch, compiler_params=_params("arbitrary"), name="s5_fwd",
    )(usc, us, jnp.asarray(perm, BF16), bd[0], cd[0], a_row[0])

    blk = lambda j: jnp.where(j < n_c, n_c - 1 - j, n_c + n_all - 1 - j)
    return pl.pallas_call(
        functools.partial(_s5_bwd_body, n_c=n_c, scan=scan), grid=(n_all,),
        in_specs=[tm_rows(blk), tm_rows(blk), _const_spec(perm.shape), _const_spec(bd.shape[1:]),
                  _const_spec(cd.shape[1:]), _const_spec(a_row.shape[1:]), _const_spec(dsk.shape),
                  _const_spec(wglu.shape)],
        out_specs=[seg(lambda j: jnp.maximum(n_c - 1 - j, 0)), seg(lambda j: n_l - 1 - jnp.maximum(j - n_c, 0))],
        out_shape=[jax.ShapeDtypeStruct(usc.shape, BF16), jax.ShapeDtypeStruct(us.shape, BF16)],
        scratch_shapes=scratch, compiler_params=_params("arbitrary"), name="s5_bwd",
    )(ut, yf, jnp.asarray(perm.T, BF16), bd[1], cd[1], a_row[1], dsk, wglu)


def _s5_matrices(abr, abi, bbr, bbi, c_re, c_im):
    g, p, h = S5_GROUPS, S5_STATE, S5_GROUP_CH
    eye = jnp.eye(g, dtype=F32)
    to_bd = lambda t: jnp.einsum("dgph,gk->dghkp", t, eye).reshape(2, g * h, g * p)
    bd = jnp.concatenate([to_bd(bbr), to_bd(bbi)], axis=-1)
    to_cd = lambda t: jnp.einsum("dghp,gk->dgpkh", t, eye).reshape(2, g * p, g * h)
    cd = jnp.concatenate([to_cd(c_re.astype(F32)), to_cd(-c_im.astype(F32))], axis=1)
    a_row = jnp.concatenate([abr.reshape(2, 1, g * p), abi.reshape(2, 1, g * p)], axis=-1)
    return bd.astype(BF16), cd.astype(BF16), a_row


def _fnet_body(u_ref, dft_ref, ccs_ref, w_ref, b_ref, o_ref, uc_ref):
    n = u_ref.shape[0]

    @pl.when(pl.program_id(1) == 0)
    def _():
        t = jnp.dot(u_ref[...], ccs_ref[...], preferred_element_type=F32)
        uc_ref[0:n, :] = t[:, :GW].astype(BF16)
        uc_ref[n:2 * n, :] = t[:, GW:].astype(BF16)

    f = _dot2(dft_ref[...], uc_ref[...])
    o = jnp.dot(f.astype(BF16), w_ref[...], preferred_element_type=F32) + b_ref[...]
    o_ref[...] = o.astype(o_ref.dtype)


def _fnet(u, dft, ccs, w, bias, tk):
    b, n, _ = u.shape
    return pl.pallas_call(
        _fnet_body, grid=(b, n // tk),
        in_specs=[pl.BlockSpec((None, n, GW), lambda bi, k: (bi, 0, 0)),
                  pl.BlockSpec((tk, 2 * n), lambda bi, k: (k, 0)),
                  _const_spec(ccs.shape), _const_spec(w.shape), _const_spec(bias.shape)],
        out_specs=pl.BlockSpec((None, tk, GW), lambda bi, k: (bi, k, 0)),
        out_shape=jax.ShapeDtypeStruct((b, n, GW), BF16),
        scratch_shapes=[pltpu.VMEM((2 * n, GW), BF16)],
        compiler_params=_params("parallel", "arbitrary"), name="fnet",
    )(u, dft, ccs, w, bias)


def _dft_matrices(n):
    ch = GW // FN_GROUPS
    k = jnp.arange(n)
    ang = (2.0 * math.pi / n) * ((k[:, None] * k[None, :]) % n).astype(F32)
    dft = jnp.concatenate([jnp.cos(ang), -jnp.sin(ang)], axis=1).astype(BF16)
    c = jnp.arange(GW)
    same = (c[:, None] // ch) == (c[None, :] // ch)
    angc = (2.0 * math.pi / ch) * (((c[:, None] % ch) * (c[None, :] % ch)) % ch).astype(F32)
    norm = 1.0 / math.sqrt(n * ch)
    cc = jnp.where(same, jnp.cos(angc), 0.0) * norm
    sc = jnp.where(same, jnp.sin(angc), 0.0) * norm
    return dft, jnp.concatenate([cc, sc], axis=1).astype(BF16)


def _out_mlp_body(x_ref, oa_ref, ob_ref, on_ref, od_ref, g1_ref, sh2_ref, sc2_ref, g2_ref,
                  wout_ref, gpm_ref, gpre_ref, gpost_ref, w1_ref, w2_ref, o_ref, *, ff_chunk):
    cat = jnp.concatenate([oa_ref[...], ob_ref[...], on_ref[...], od_ref[...]], axis=-1)
    y = jnp.dot(cat, wout_ref[...], preferred_element_type=F32)
    x1 = x_ref[...] + g1_ref[...] * _rms(y, gpm_ref[...])
    h = (_rms(x1, gpre_ref[...]) * (1.0 + sc2_ref[...]) + sh2_ref[...]).astype(BF16)
    d_ff = w1_ref.shape[1]
    m = jnp.zeros(x1.shape, F32)
    for c0 in range(0, d_ff, ff_chunk):
        a = jnp.maximum(jnp.dot(h, w1_ref[:, c0:c0 + ff_chunk], preferred_element_type=F32), 0.0)
        m = m + jnp.dot((a * a).astype(BF16), w2_ref[c0:c0 + ff_chunk, :], preferred_element_type=F32)
    o_ref[...] = x1 + g2_ref[...] * _rms(m, gpost_ref[...])


def _out_mlp(x, oa, ob, on, od, mods, wout, gpm, gpre, gpost, w1, w2, tm):
    b, n, d = x.shape
    row = lambda w: pl.BlockSpec((None, tm, w), lambda bi, i: (bi, i, 0))
    mod = pl.BlockSpec((None, 1, d), lambda bi, i: (bi, 0, 0))
    consts = [wout, gpm, gpre, gpost, w1, w2]
    return pl.pallas_call(
        functools.partial(_out_mlp_body, ff_chunk=1024),
        grid=(b, n // tm),
        in_specs=[row(d), row(GW), row(GW), row(GW), row(GW), mod, mod, mod, mod]
        + [_const_spec(t.shape) for t in consts],
        out_specs=row(d), out_shape=jax.ShapeDtypeStruct((b, n, d), F32),
        compiler_params=_params("parallel", "parallel"), name="out_mlp",
    )(x, oa, ob, on, od, *mods, *consts)


def _rope_tables(n):
    half = HEAD_DIM // 4
    pos = jnp.arange(n, dtype=jnp.int32)
    row = (pos // GRID_W).astype(F32)
    col = (pos % GRID_W).astype(F32)
    lane = jnp.arange(GW)
    jj = lane % HEAD_DIM
    inv = ROPE_BASE ** (-(jj % half).astype(F32) / half)
    p = jnp.where((jj // (2 * half) == 0)[None, :], row[:, None], col[:, None])
    ang = p * inv[None, :]
    first = ((jj % (2 * half)) < half)[None, :]
    sin = jnp.sin(ang)
    return jnp.cos(ang), jnp.where(first, -sin, 0.0), jnp.where(first, 0.0, sin)


def kernel(x, c, ctx, c_ctx, w_ada, b_ada, g_pre_mix, g_post_mix, g_pre_mlp, g_post_mlp, w_in, g_q_attn, g_k_attn,
           s5_a_re, s5_a_im, s5_log_dt, s5_b_re, s5_b_im, s5_c_re, s5_c_im, s5_d, w_s5_glu, na_rel_bias,
           w_fnet, b_fnet, w_out, w_mlp1, w_mlp2):
    b, n, d = x.shape
    lc = ctx.shape[1]
    depth = w_ada.shape[0]
    rows = n // GRID_W

    n_rows = -(-(b + 1) // 8) * 8
    cc = jnp.concatenate([c, c_ctx[None, :], jnp.zeros((n_rows - b - 1, d), c.dtype)], axis=0)
    mod_all = _ada(cc, w_ada, b_ada).reshape(depth, n_rows, N_MOD, d)

    abr, abi, bbr, bbi = _s5_prep(s5_a_re, s5_a_im, s5_log_dt, s5_b_re, s5_b_im)
    rope_tabs = _rope_tables(n)
    lane = jnp.arange(GW)
    hm = jnp.where((lane[:, None] // HEAD_DIM) == (lane[None, :] // HEAD_DIM), 1.0 / HEAD_DIM, 0.0).astype(BF16)
    dft_n, ccs_n = _dft_matrices(n)
    dft_c, ccs_c = _dft_matrices(lc)

    tm_in = min(512, n)
    tm_c = min(256, lc)
    tc = math.gcd(64, math.gcd(n, lc))

    xc = ctx
    for l in range(depth):
        need_ctx = l < depth - 1
        mods = [mod_all[l, :b, i][:, None, :] for i in range(N_MOD)]
        mods_c = [jnp.broadcast_to(mod_all[l, b, i][None, None, :], (b, 1, d)) for i in range(N_MOD)]
        vec = lambda t: t[l].reshape(1, -1).astype(F32)
        w_in_l = w_in[l].astype(BF16)
        gq = jnp.tile(g_q_attn[l].astype(F32), ATT_HEADS)[None, :]
        gk = jnp.tile(g_k_attn[l].astype(F32), ATT_KV_HEADS)[None, :]

        qa, ka, va, us, nq, nk, nv, fn = _in_proj(x, mods[0], mods[1], vec(g_pre_mix), w_in_l, gq, gk, hm,
                                                  rope_tabs, tm_in)
        qac, kac, vac, usc, nqc, nkc, nvc, fnc = _in_proj(xc, mods_c[0], mods_c[1], vec(g_pre_mix), w_in_l,
                                                          gq, gk, hm, None, tm_c)

        oa = _mha(qa, ka, va, kac, vac, group=ATT_HEADS // ATT_KV_HEADS, tq=min(256, n), v_ones=True)
        on = _na(nq, nk, nv, nkc, nvc, _na_bias_table(na_rel_bias[l], rows), rt=min(4, rows))
        w_fn = w_fnet[l].astype(BF16)
        od = _fnet(fn, dft_n, ccs_n, w_fn, vec(b_fnet), tk=min(512, n))

        bd, cd, a_row = _s5_matrices(abr[l], abi[l], bbr[l], bbi[l], s5_c_re[l], s5_c_im[l])
        obc, ob = _s5(usc, us, bd, cd, a_row, vec(s5_d), w_s5_glu[l].astype(BF16), tc)

        consts = (w_out[l].astype(BF16), vec(g_post_mix), vec(g_pre_mlp), vec(g_post_mlp),
                  w_mlp1[l].astype(BF16), w_mlp2[l].astype(BF16))
        x_new = _out_mlp(x, oa, ob, on, od, (mods[2], mods[3], mods[4], mods[5]), *consts, tm_in)
        if need_ctx:
            oac = _mha(qac, kac, vac, group=ATT_HEADS // ATT_KV_HEADS, tq=tm_c, v_ones=True)
            onc = _mha(nqc, nkc, nvc, group=1, tq=tm_c, v_ones=False)
            odc = _fnet(fnc, dft_c, ccs_c, w_fn, vec(b_fnet), tk=tm_c)
            xc = _out_mlp(xc, oac, obc, onc, odc, (mods_c[2], mods_c[3], mods_c[4], mods_c[5]), *consts, tm_c)
        x = x_new
    return x
```

```python
import functools
import math

import jax
import jax.numpy as jnp
import numpy as np
from jax import lax
from jax.experimental import pallas as pl
from jax.experimental.pallas import tpu as pltpu

F32 = jnp.float32
BF16 = jnp.bfloat16
HIGHEST = lax.Precision.HIGHEST

HEAD_DIM = 64
GRID_W = 64
N_MOD = 6
EPS = 1e-6
ATT_HEADS = 4
ATT_KV_HEADS = 2
ROPE_BASE = 10000.0
S5_GROUPS = 16
S5_GROUP_CH = 16
S5_STATE = 64
S5_MIN_DECAY = 1e-4
NA_HEADS = 4
NA_ROWS = 8
NA_COLS = 16
FN_GROUPS = 4
GW = 256
MASK_VALUE = -1e30
LOG2E = 1.4426950408889634
KEY_CHUNK = 1024

OFF_ATT_Q, OFF_ATT_K, OFF_ATT_V, OFF_S5 = 0, 256, 384, 512
OFF_NA_Q, OFF_NA_K, OFF_NA_V, OFF_FN, IN_WIDTH = 768, 1024, 1280, 1536, 1792

VMEM_LIMIT = 56 * 1024 * 1024


def _params(*sem):
    return pltpu.CompilerParams(dimension_semantics=sem, vmem_limit_bytes=VMEM_LIMIT)


def _const_spec(shape):
    return pl.BlockSpec(shape, lambda *_: (0,) * len(shape), pipeline_mode=pl.Buffered(1))


def _rms(x, g):
    return x * lax.rsqrt(jnp.mean(x * x, axis=-1, keepdims=True) + EPS) * g


def _dot2(a, b):
    half = a.shape[0] // 2
    return jnp.concatenate([jnp.dot(a[:half], b, preferred_element_type=F32),
                            jnp.dot(a[half:], b, preferred_element_type=F32)], axis=0)


def _nt_dot(a, b):
    return lax.dot_general(a, b, (((1,), (1,)), ((), ())), preferred_element_type=F32)


def _ada_body(c_ref, w_ref, b_ref, o_ref):
    c = c_ref[...]
    s = c * jax.nn.sigmoid(c)
    o_ref[...] = jnp.dot(s, w_ref[...], precision=HIGHEST, preferred_element_type=F32) + b_ref[...]


def _ada(cc, w_ada, b_ada, tn=1536):
    depth, d, n_out = w_ada.shape
    rows = cc.shape[0]
    return pl.pallas_call(
        _ada_body,
        grid=(depth, n_out // tn),
        in_specs=[
            pl.BlockSpec((rows, d), lambda l, j: (0, 0)),
            pl.BlockSpec((None, d, tn), lambda l, j: (l, 0, j)),
            pl.BlockSpec((None, 1, tn), lambda l, j: (l, 0, j)),
        ],
        out_specs=pl.BlockSpec((None, rows, tn), lambda l, j: (l, 0, j)),
        out_shape=jax.ShapeDtypeStruct((depth, rows, n_out), F32),
        compiler_params=_params("parallel", "parallel"),
        name="ada",
    )(cc, w_ada, b_ada.reshape(depth, 1, n_out))


def _s5_prep_body(are_ref, aim_ref, ldt_ref, are_b_ref, aim_b_ref, ldt_b_ref, bre_ref, bim_ref,
                  abr_ref, abi_ref, bbr_ref, bbi_ref):
    def zoh(a_re, a_im, log_dt):
        lr = jnp.minimum(a_re, -S5_MIN_DECAY)
        dt = jnp.exp(log_dt)
        mag = jnp.exp(lr * dt)
        ab_r = mag * jnp.cos(a_im * dt)
        ab_i = mag * jnp.sin(a_im * dt)
        return lr, ab_r, ab_i

    _, ab_r, ab_i = zoh(are_ref[...], aim_ref[...], ldt_ref[...])
    abr_ref[...] = ab_r
    abi_ref[...] = ab_i
    lr, ab_r, ab_i = zoh(are_b_ref[...], aim_b_ref[...], ldt_b_ref[...])
    li = aim_b_ref[...]
    nr, ni = ab_r - 1.0, ab_i
    den = lr * lr + li * li
    kr = (nr * lr + ni * li) / den
    ki = (ni * lr - nr * li) / den
    br, bi = bre_ref[...], bim_ref[...]
    bbr_ref[...] = kr * br - ki * bi
    bbi_ref[...] = kr * bi + ki * br


def _s5_prep(a_re, a_im, log_dt, b_re, b_im):
    lead = a_re.shape[:3]
    r = lead[0] * lead[1] * lead[2]
    p, h = S5_STATE, S5_GROUP_CH
    a2 = lambda t: t.reshape(r, p)
    ab = lambda t: jnp.broadcast_to(t.reshape(r, p, 1), (r, p, h)).reshape(r, p * h)
    ldt = jnp.broadcast_to(log_dt.reshape(r, 1), (r, p))
    ldt_b = jnp.broadcast_to(log_dt.reshape(r, 1), (r, p * h))
    small = jax.ShapeDtypeStruct((r, p), F32)
    big = jax.ShapeDtypeStruct((r, p * h), F32)
    abr, abi, bbr, bbi = pl.pallas_call(
        _s5_prep_body, out_shape=(small, small, big, big), name="s5_prep",
    )(a2(a_re), a2(a_im), ldt, ab(a_re), ab(a_im), ldt_b, b_re.reshape(r, p * h), b_im.reshape(r, p * h))
    return (abr.reshape(*lead, p), abi.reshape(*lead, p),
            bbr.reshape(*lead, p, h), bbi.reshape(*lead, p, h))


def _in_proj_body(*refs, rope):
    if rope:
        (x_ref, sh_ref, sc_ref, g_ref, w_ref, gq_ref, gk_ref, hm_ref, cos_ref, sa_ref, sb_ref,
         qa_ref, ka_ref, va_ref, us_ref, nq_ref, nk_ref, nv_ref, fn_ref) = refs
    else:
        (x_ref, sh_ref, sc_ref, g_ref, w_ref, gq_ref, gk_ref, hm_ref,
         qa_ref, ka_ref, va_ref, us_ref, nq_ref, nk_ref, nv_ref, fn_ref) = refs
    h = _rms(x_ref[...], g_ref[...]) * (1.0 + sc_ref[...]) + sh_ref[...]
    hb = h.astype(BF16)
    z_lo = jnp.dot(hb, w_ref[:, :OFF_NA_K], preferred_element_type=F32)
    z_hi = jnp.dot(hb, w_ref[:, OFF_NA_K:], preferred_element_type=F32)
    hm = hm_ref[...]

    def head_norm(t, g, avg):
        ms = jnp.dot((t * t).astype(BF16), avg, preferred_element_type=F32)
        return t * lax.rsqrt(ms + EPS) * g

    q = head_norm(z_lo[:, OFF_ATT_Q:OFF_ATT_K], gq_ref[...], hm)
    kw = OFF_ATT_V - OFF_ATT_K
    k = head_norm(z_lo[:, OFF_ATT_K:OFF_ATT_V], gk_ref[...], hm[:kw, :kw])
    if rope:
        def rot(t):
            w = t.shape[-1]
            half = HEAD_DIM // 4
            return (t * cos_ref[:, :w] + pltpu.roll(t, w - half, 1) * sa_ref[:, :w]
                    + pltpu.roll(t, half, 1) * sb_ref[:, :w])
        q, k = rot(q), rot(k)
    scale = HEAD_DIM ** -0.5 * LOG2E
    qa_ref[...] = (q * scale).astype(qa_ref.dtype)
    ka_ref[...] = k.astype(ka_ref.dtype)
    v = z_lo[:, OFF_ATT_V:OFF_S5]
    lane = lax.broadcasted_iota(jnp.int32, (v.shape[0], HEAD_DIM), 1)
    one_col = jnp.where(lane == 0, 1.0, 0.0)
    v_ext = jnp.concatenate(
        sum(([v[:, HEAD_DIM * i:HEAD_DIM * (i + 1)], one_col] for i in range(ATT_KV_HEADS)), []), axis=-1)
    va_ref[...] = v_ext.T.astype(va_ref.dtype)
    us_ref[...] = z_lo[:, OFF_S5:OFF_NA_Q].astype(us_ref.dtype)
    nq_ref[...] = (z_lo[:, OFF_NA_Q:OFF_NA_K] * scale).astype(nq_ref.dtype)
    nk_ref[...] = z_hi[:, :GW].astype(nk_ref.dtype)
    nv_ref[...] = z_hi[:, GW:2 * GW].astype(nv_ref.dtype)
    fn_ref[...] = z_hi[:, 2 * GW:].astype(fn_ref.dtype)


def _in_proj(x, sh, sc, g_pre, w_in, gq, gk, hm, rope_tabs, tm):
    b, n, d = x.shape
    rope = rope_tabs is not None
    row = lambda w: pl.BlockSpec((None, tm, w), lambda bi, i: (bi, i, 0))
    mod = pl.BlockSpec((None, 1, d), lambda bi, i: (bi, 0, 0))
    in_specs = [row(d), mod, mod, _const_spec((1, d)), _const_spec(w_in.shape),
                _const_spec(gq.shape), _const_spec(gk.shape), _const_spec(hm.shape)]
    args = [x, sh, sc, g_pre, w_in, gq, gk, hm]
    if rope:
        in_specs += [pl.BlockSpec((tm, GW), lambda bi, i: (i, 0))] * 3
        args += list(rope_tabs)
    kvw = ATT_KV_HEADS * HEAD_DIM
    widths = [GW, kvw, None, GW, GW, GW, GW, GW]
    out_specs = [pl.BlockSpec((None, 2 * kvw, tm), lambda bi, i: (bi, 0, i)) if w is None else row(w) for w in widths]
    out_shape = [jax.ShapeDtypeStruct((b, 2 * kvw, n) if w is None else (b, n, w), BF16) for w in widths]
    return pl.pallas_call(
        functools.partial(_in_proj_body, rope=rope),
        grid=(b, n // tm), in_specs=in_specs, out_specs=out_specs, out_shape=out_shape,
        compiler_params=_params("parallel", "parallel"), name="in_proj_rope" if rope else "in_proj",
    )(*args)


def _col_max(s):
    for rows in (256, 64):
        if s.shape[0] > rows and s.shape[0] % rows == 0:
            s = jnp.max(s.reshape(s.shape[0] // rows, rows, s.shape[1]), axis=0)
    return jnp.max(s, axis=0, keepdims=True)


def _gqa_body(*refs, n_heads, group, two_sets):
    if two_sets:
        q_ref, k1_ref, v1t_ref, k2_ref, v2t_ref, o_ref = refs
    else:
        q_ref, k1_ref, v1t_ref, o_ref = refs
    key_sets = [(k1_ref, v1t_ref)] + ([(k2_ref, v2t_ref)] if two_sets else [])

    def scores(h):
        ks = slice(HEAD_DIM * (h // group), HEAD_DIM * (h // group + 1))
        qh = q_ref[:, HEAD_DIM * h:HEAD_DIM * (h + 1)]
        s = [_nt_dot(k_ref[:, ks], qh) for k_ref, _ in key_sets]
        m = functools.reduce(jnp.maximum, [_col_max(t) for t in s])
        return s, m

    def weighted_values(h, s, m):
        vr = slice(2 * HEAD_DIM * (h // group), 2 * HEAD_DIM * (h // group + 1))
        o = None
        for t, (_, vt_ref) in zip(s, key_sets):
            for c0 in range(0, t.shape[0], KEY_CHUNK):
                p = jnp.exp2(t[c0:c0 + KEY_CHUNK] - m).astype(BF16)
                part = jnp.dot(vt_ref[vr, c0:c0 + KEY_CHUNK], p, preferred_element_type=F32)
                o = part if o is None else o + part
        return o[:HEAD_DIM] / o[HEAD_DIM:HEAD_DIM + 1]

    outs = []
    nxt = scores(0)
    for h in range(n_heads):
        cur, nxt = nxt, (scores(h + 1) if h + 1 < n_heads else None)
        outs.append(weighted_values(h, *cur))
    o_ref[...] = jnp.concatenate(outs, axis=0).T.astype(o_ref.dtype)


def _gqa(q, k1, v1t, k2=None, v2t=None, *, group, tq):
    b, nq, qw = q.shape
    two_sets = k2 is not None
    full = lambda t: pl.BlockSpec((None,) + t.shape[1:], lambda bi, i: (bi, 0, 0))
    args = [q, k1, v1t] + ([k2, v2t] if two_sets else [])
    in_specs = [pl.BlockSpec((None, tq, qw), lambda bi, i: (bi, i, 0))] + [full(t) for t in args[1:]]
    return pl.pallas_call(
        functools.partial(_gqa_body, n_heads=qw // HEAD_DIM, group=group, two_sets=two_sets),
        grid=(b, nq // tq), in_specs=in_specs,
        out_specs=pl.BlockSpec((None, tq, qw), lambda bi, i: (bi, i, 0)),
        out_shape=jax.ShapeDtypeStruct((b, nq, qw), BF16),
        compiler_params=_params("parallel", "parallel"), name="gqa2" if two_sets else "gqa1",
    )(*args)


def _mha_body(q_ref, k_ref, v_ref, o_ref, *, n_heads):
    for h in range(n_heads):
        hs = slice(HEAD_DIM * h, HEAD_DIM * (h + 1))
        s = _nt_dot(q_ref[:, hs], k_ref[:, hs])
        p = jnp.exp2(s - jnp.max(s, axis=-1, keepdims=True))
        o = jnp.dot(p.astype(BF16), v_ref[:, hs], preferred_element_type=F32)
        o_ref[:, hs] = (o / jnp.sum(p, axis=-1, keepdims=True)).astype(o_ref.dtype)


def _mha(q, k, v, *, tq):
    b, nq, qw = q.shape
    full = lambda t: pl.BlockSpec((None,) + t.shape[1:], lambda bi, i: (bi, 0, 0))
    return pl.pallas_call(
        functools.partial(_mha_body, n_heads=qw // HEAD_DIM),
        grid=(b, nq // tq),
        in_specs=[pl.BlockSpec((None, tq, qw), lambda bi, i: (bi, i, 0)), full(k), full(v)],
        out_specs=pl.BlockSpec((None, tq, qw), lambda bi, i: (bi, i, 0)),
        out_shape=jax.ShapeDtypeStruct((b, nq, qw), BF16),
        compiler_params=_params("parallel", "parallel"), name="mha",
    )(q, k, v)


def _na_body(q_ref, k_ref, v_ref, kc_ref, vc_ref, bias_ref, o_ref, *, rt, k_r, rows):
    i = pl.program_id(1)
    kc = kc_ref[...]
    vc = vc_ref[...]
    lane_head = lax.broadcasted_iota(jnp.int32, (GRID_W, GW), 1) // HEAD_DIM
    for j in range(rt):
        r = i * rt + j
        rs = jnp.clip(r - k_r // 2, 0, rows - k_r)
        start = pl.multiple_of(rs * GRID_W, GRID_W)
        kw = k_ref[pl.ds(start, k_r * GRID_W), :]
        vw = v_ref[pl.ds(start, k_r * GRID_W), :]
        q = q_ref[j * GRID_W:(j + 1) * GRID_W, :]
        q4 = jnp.concatenate([jnp.where(lane_head == h, q, jnp.zeros_like(q)) for h in range(NA_HEADS)], axis=0)
        s_loc = _nt_dot(q4, kw) + bias_ref[r - rs]
        s_ctx = _nt_dot(q4, kc)
        m = jnp.maximum(jnp.max(s_loc, axis=-1, keepdims=True), jnp.max(s_ctx, axis=-1, keepdims=True))
        p_loc = jnp.exp2(s_loc - m)
        p_ctx = jnp.exp2(s_ctx - m)
        l = jnp.sum(p_loc, axis=-1, keepdims=True) + jnp.sum(p_ctx, axis=-1, keepdims=True)
        o4 = (jnp.dot(p_loc.astype(BF16), vw, preferred_element_type=F32)
              + jnp.dot(p_ctx.astype(BF16), vc, preferred_element_type=F32)) / l
        o = jnp.zeros((GRID_W, GW), F32)
        for h in range(NA_HEADS):
            o = o + jnp.where(lane_head == h, o4[h * GRID_W:(h + 1) * GRID_W, :], 0.0)
        o_ref[j * GRID_W:(j + 1) * GRID_W, :] = o.astype(o_ref.dtype)


def _na(q, k, v, kc, vc, bias, rt):
    b, n, _ = q.shape
    rows = n // GRID_W
    k_r = bias.shape[0]
    full = lambda t: pl.BlockSpec((None,) + t.shape[1:], lambda bi, i: (bi, 0, 0))
    return pl.pallas_call(
        functools.partial(_na_body, rt=rt, k_r=k_r, rows=rows),
        grid=(b, rows // rt),
        in_specs=[pl.BlockSpec((None, rt * GRID_W, GW), lambda bi, i: (bi, i, 0)),
                  full(k), full(v), full(kc), full(vc), _const_spec(bias.shape)],
        out_specs=pl.BlockSpec((None, rt * GRID_W, GW), lambda bi, i: (bi, i, 0)),
        out_shape=jax.ShapeDtypeStruct((b, n, GW), BF16),
        compiler_params=_params("parallel", "parallel"), name="na",
    )(q, k, v, kc, vc, bias)


def _na_bias_table(rel_bias, rows):
    k_r = min(NA_ROWS, rows)
    cols = np.arange(GRID_W)
    col_start = np.clip(cols - NA_COLS // 2, 0, GRID_W - NA_COLS)
    inside = (cols[None, :] >= col_start[:, None]) & (cols[None, :] < col_start[:, None] + NA_COLS)
    rel_c = cols[None, :] - cols[:, None] + (NA_COLS - 1)
    rel_r = np.arange(k_r)[None, :] - np.arange(k_r)[:, None] + (NA_ROWS - 1)
    pick_r = (rel_r[:, :, None] == np.arange(2 * NA_ROWS - 1)).astype(np.float32)
    pick_c = ((rel_c[:, :, None] == np.arange(2 * NA_COLS - 1)) & inside[:, :, None]).astype(np.float32)
    t = jnp.einsum("cai,him->cham", pick_r, rel_bias.astype(F32), precision=HIGHEST)
    t = jnp.einsum("cham,jkm->chjak", t, pick_c, precision=HIGHEST)
    t = t + np.where(inside, 0.0, MASK_VALUE).astype(np.float32)[None, None, :, None, :]
    return (t * LOG2E).reshape(k_r, NA_HEADS * GRID_W, k_r * GRID_W)


def _s5_scan(hs_ref, st_ref, a_ref, *, tc, nb, n_state, col_w, reverse):
    for c0 in range(0, n_state, col_w):
        re = slice(c0, c0 + col_w)
        im = slice(n_state + c0, n_state + c0 + col_w)
        ar = jnp.broadcast_to(a_ref[:, re], (nb, col_w))
        ai = jnp.broadcast_to(a_ref[:, im], (nb, col_w))

        def step(t, carry):
            hr, hi = carry
            rows = pl.ds(pl.multiple_of((tc - 1 - t if reverse else t) * nb, nb), nb)
            nr = ar * hr - ai * hi + hs_ref[rows, re]
            ni = ar * hi + ai * hr + hs_ref[rows, im]
            hs_ref[rows, re] = nr
            hs_ref[rows, im] = ni
            return nr, ni

        hr, hi = lax.fori_loop(0, tc, step, (st_ref[:, re], st_ref[:, im]), unroll=2)
        st_ref[:, re] = hr
        st_ref[:, im] = hi


def _s5_fwd_body(uc_ref, ul_ref, perm_ref, bd_ref, cd_ref, a_ref, ut_ref, yf_ref, hs_ref, st_ref, *, n_c, scan):
    j = pl.program_id(0)

    @pl.when(j == 0)
    def _():
        st_ref[...] = jnp.zeros_like(st_ref)

    def to_time_major(src_ref):
        nb, tc, w = src_ref.shape
        ut_ref[...] = _dot2(perm_ref[...], src_ref[...].reshape(nb * tc, w)).astype(ut_ref.dtype)

    pl.when(j < n_c)(lambda: to_time_major(uc_ref))
    pl.when(j >= n_c)(lambda: to_time_major(ul_ref))
    hs_ref[...] = _dot2(ut_ref[...], bd_ref[...])
    scan(hs_ref, st_ref, a_ref, reverse=False)
    yf_ref[...] = _dot2(hs_ref[...].astype(BF16), cd_ref[...])


def _s5_bwd_body(ut_ref, yf_ref, permt_ref, bd_ref, cd_ref, a_ref, dsk_ref, wglu_ref, oc_ref, ol_ref,
                 hs_ref, st_ref, *, n_c, scan):
    j = pl.program_id(0)

    @pl.when(j == 0)
    def _():
        st_ref[...] = jnp.zeros_like(st_ref)

    u = ut_ref[...]
    hs_ref[...] = _dot2(u, bd_ref[...])
    scan(hs_ref, st_ref, a_ref, reverse=True)
    t = dsk_ref[...] * u.astype(F32) + yf_ref[...] + _dot2(hs_ref[...].astype(BF16), cd_ref[...])
    g = jax.nn.gelu(t)
    ob = g * jax.nn.sigmoid(_dot2(g.astype(BF16), wglu_ref[...]))
    ob = _dot2(permt_ref[...], ob.astype(BF16)).astype(BF16)

    @pl.when(j < n_c)
    def _():
        oc_ref[...] = ob.reshape(oc_ref.shape)

    @pl.when(j >= n_c)
    def _():
        ol_ref[...] = ob.reshape(ol_ref.shape)


def _s5(usc, us, bd, cd, a_row, dsk, wglu, tc):
    nb, lc, _ = usc.shape
    n = us.shape[1]
    n_c, n_l = lc // tc, n // tc
    n_all = n_c + n_l
    n_state2 = bd.shape[-1]
    rows = tc * nb
    r = np.arange(rows)
    perm = np.zeros((rows, rows), np.float32)
    perm[r, (r % nb) * tc + r // nb] = 1.0
    scan = functools.partial(_s5_scan, tc=tc, nb=nb, n_state=n_state2 // 2, col_w=512)
    scratch = [pltpu.VMEM((rows, n_state2), F32), pltpu.VMEM((nb, n_state2), F32)]
    seg = lambda idx: pl.BlockSpec((nb, tc, GW), lambda j: (0, idx(j), 0))
    tm_rows = lambda idx: pl.BlockSpec((rows, GW), lambda j: (idx(j), 0))

    ut, yf = pl.pallas_call(
        functools.partial(_s5_fwd_body, n_c=n_c, scan=scan), grid=(n_all,),
        in_specs=[seg(lambda j: jnp.minimum(j, n_c - 1)), seg(lambda j: jnp.maximum(j - n_c, 0)),
                  _const_spec(perm.shape), _const_spec(bd.shape[1:]), _const_spec(cd.shape[1:]),
                  _const_spec(a_row.shape[1:])],
        out_specs=[tm_rows(lambda j: j), tm_rows(lambda j: j)],
        out_shape=[jax.ShapeDtypeStruct((n_all * rows, GW), BF16), jax.ShapeDtypeStruct((n_all * rows, GW), F32)],
        scratch_shapes=scratch, compiler_params=_params("arbitrary"), name="s5_fwd",
    )(usc, us, jnp.asarray(perm, BF16), bd[0], cd[0], a_row[0])

    blk = lambda j: jnp.where(j < n_c, n_c - 1 - j, n_c + n_all - 1 - j)
    return pl.pallas_call(
        functools.partial(_s5_bwd_body, n_c=n_c, scan=scan), grid=(n_all,),
        in_specs=[tm_rows(blk), tm_rows(blk), _const_spec(perm.shape), _const_spec(bd.shape[1:]),
                  _const_spec(cd.shape[1:]), _const_spec(a_row.shape[1:]), _const_spec(dsk.shape),
                  _const_spec(wglu.shape)],
        out_specs=[seg(lambda j: jnp.maximum(n_c - 1 - j, 0)), seg(lambda j: n_l - 1 - jnp.maximum(j - n_c, 0))],
        out_shape=[jax.ShapeDtypeStruct(usc.shape, BF16), jax.ShapeDtypeStruct(us.shape, BF16)],
        scratch_shapes=scratch, compiler_params=_params("arbitrary"), name="s5_bwd",
    )(ut, yf, jnp.asarray(perm.T, BF16), bd[1], cd[1], a_row[1], dsk, wglu)


def _s5_matrices(abr, abi, bbr, bbi, c_re, c_im):
    g, p, h = S5_GROUPS, S5_STATE, S5_GROUP_CH
    eye = jnp.eye(g, dtype=F32)
    to_bd = lambda t: jnp.einsum("dgph,gk->dghkp", t, eye).reshape(2, g * h, g * p)
    bd = jnp.concatenate([to_bd(bbr), to_bd(bbi)], axis=-1)
    to_cd = lambda t: jnp.einsum("dghp,gk->dgpkh", t, eye).reshape(2, g * p, g * h)
    cd = jnp.concatenate([to_cd(c_re.astype(F32)), to_cd(-c_im.astype(F32))], axis=1)
    a_row = jnp.concatenate([abr.reshape(2, 1, g * p), abi.reshape(2, 1, g * p)], axis=-1)
    return bd.astype(BF16), cd.astype(BF16), a_row


def _fnet_body(u_ref, dft_ref, ccs_ref, w_ref, b_ref, o_ref, uc_ref):
    n = u_ref.shape[0]

    @pl.when(pl.program_id(1) == 0)
    def _():
        t = jnp.dot(u_ref[...], ccs_ref[...], preferred_element_type=F32)
        uc_ref[0:n, :] = t[:, :GW].astype(BF16)
        uc_ref[n:2 * n, :] = t[:, GW:].astype(BF16)

    tk = o_ref.shape[0]
    rows = pl.ds(pl.multiple_of(pl.program_id(1) * tk, tk), tk)
    f = _dot2(dft_ref[rows, :], uc_ref[...])
    o = jnp.dot(f.astype(BF16), w_ref[...], preferred_element_type=F32) + b_ref[...]
    o_ref[...] = o.astype(o_ref.dtype)


def _fnet(u, dft, ccs, w, bias, tk):
    b, n, _ = u.shape
    return pl.pallas_call(
        _fnet_body, grid=(b, n // tk),
        in_specs=[pl.BlockSpec((None, n, GW), lambda bi, k: (bi, 0, 0)), _const_spec(dft.shape),
                  _const_spec(ccs.shape), _const_spec(w.shape), _const_spec(bias.shape)],
        out_specs=pl.BlockSpec((None, tk, GW), lambda bi, k: (bi, k, 0)),
        out_shape=jax.ShapeDtypeStruct((b, n, GW), BF16),
        scratch_shapes=[pltpu.VMEM((2 * n, GW), BF16)],
        compiler_params=_params("parallel", "arbitrary"), name="fnet",
    )(u, dft, ccs, w, bias)


def _dft_matrices(n):
    ch = GW // FN_GROUPS
    lo = 64
    k = jnp.arange(n)[:, None]
    ang_a = (2.0 * math.pi * lo / n) * ((k * jnp.arange(n // lo)[None, :]) % (n // lo)).astype(F32)
    ang_b = (2.0 * math.pi / n) * ((k * jnp.arange(lo)[None, :]) % n).astype(F32)
    ca, sa = jnp.cos(ang_a)[:, :, None], jnp.sin(ang_a)[:, :, None]
    cb, sb = jnp.cos(ang_b)[:, None, :], jnp.sin(ang_b)[:, None, :]
    dft = jnp.stack([ca * cb - sa * sb, -(sa * cb + ca * sb)], axis=1).reshape(n, 2 * n).astype(BF16)
    c = jnp.arange(GW)
    same = (c[:, None] // ch) == (c[None, :] // ch)
    angc = (2.0 * math.pi / ch) * (((c[:, None] % ch) * (c[None, :] % ch)) % ch).astype(F32)
    norm = 1.0 / math.sqrt(n * ch)
    cc = jnp.where(same, jnp.cos(angc), 0.0) * norm
    sc = jnp.where(same, jnp.sin(angc), 0.0) * norm
    return dft, jnp.concatenate([cc, sc], axis=1).astype(BF16)


def _out_mlp_body(x_ref, oa_ref, ob_ref, on_ref, od_ref, g1_ref, sh2_ref, sc2_ref, g2_ref,
                  wout_ref, gpm_ref, gpre_ref, gpost_ref, w1_ref, w2_ref, o_ref, *, ff_chunk):
    cat = jnp.concatenate([oa_ref[...], ob_ref[...], on_ref[...], od_ref[...]], axis=-1)
    y = jnp.dot(cat, wout_ref[...], preferred_element_type=F32)
    x1 = x_ref[...] + g1_ref[...] * _rms(y, gpm_ref[...])
    h = (_rms(x1, gpre_ref[...]) * (1.0 + sc2_ref[...]) + sh2_ref[...]).astype(BF16)
    d_ff = w1_ref.shape[1]
    m = jnp.zeros(x1.shape, F32)
    for c0 in range(0, d_ff, ff_chunk):
        a = jnp.maximum(jnp.dot(h, w1_ref[:, c0:c0 + ff_chunk], preferred_element_type=F32), 0.0)
        m = m + jnp.dot((a * a).astype(BF16), w2_ref[c0:c0 + ff_chunk, :], preferred_element_type=F32)
    o_ref[...] = x1 + g2_ref[...] * _rms(m, gpost_ref[...])


def _out_mlp(x, oa, ob, on, od, mods, wout, gpm, gpre, gpost, w1, w2, tm):
    b, n, d = x.shape
    row = lambda w: pl.BlockSpec((None, tm, w), lambda bi, i: (bi, i, 0))
    mod = pl.BlockSpec((None, 1, d), lambda bi, i: (bi, 0, 0))
    consts = [wout, gpm, gpre, gpost, w1, w2]
    return pl.pallas_call(
        functools.partial(_out_mlp_body, ff_chunk=1024),
        grid=(b, n // tm),
        in_specs=[row(d), row(GW), row(GW), row(GW), row(GW), mod, mod, mod, mod]
        + [_const_spec(t.shape) for t in consts],
        out_specs=row(d), out_shape=jax.ShapeDtypeStruct((b, n, d), F32),
        compiler_params=_params("parallel", "parallel"), name="out_mlp",
    )(x, oa, ob, on, od, *mods, *consts)


def _rope_tables(n):
    half = HEAD_DIM // 4
    pos = jnp.arange(n, dtype=jnp.int32)
    row = (pos // GRID_W).astype(F32)
    col = (pos % GRID_W).astype(F32)
    lane = jnp.arange(GW)
    jj = lane % HEAD_DIM
    inv = ROPE_BASE ** (-(jj % half).astype(F32) / half)
    p = jnp.where((jj // (2 * half) == 0)[None, :], row[:, None], col[:, None])
    ang = p * inv[None, :]
    first = ((jj % (2 * half)) < half)[None, :]
    sin = jnp.sin(ang)
    return jnp.cos(ang), jnp.where(first, -sin, 0.0), jnp.where(first, 0.0, sin)


def kernel(x, c, ctx, c_ctx, w_ada, b_ada, g_pre_mix, g_post_mix, g_pre_mlp, g_post_mlp, w_in, g_q_attn, g_k_attn,
           s5_a_re, s5_a_im, s5_log_dt, s5_b_re, s5_b_im, s5_c_re, s5_c_im, s5_d, w_s5_glu, na_rel_bias,
           w_fnet, b_fnet, w_out, w_mlp1, w_mlp2):
    b, n, d = x.shape
    lc = ctx.shape[1]
    depth = w_ada.shape[0]
    rows = n // GRID_W

    n_rows = -(-(b + 1) // 8) * 8
    cc = jnp.concatenate([c, c_ctx[None, :], jnp.zeros((n_rows - b - 1, d), c.dtype)], axis=0)
    mod_all = _ada(cc, w_ada, b_ada).reshape(depth, n_rows, N_MOD, d)

    abr, abi, bbr, bbi = _s5_prep(s5_a_re, s5_a_im, s5_log_dt, s5_b_re, s5_b_im)
    rope_tabs = _rope_tables(n)
    lane = jnp.arange(GW)
    hm = jnp.where((lane[:, None] // HEAD_DIM) == (lane[None, :] // HEAD_DIM), 1.0 / HEAD_DIM, 0.0).astype(BF16)
    dft_n, ccs_n = _dft_matrices(n)
    dft_c, ccs_c = _dft_matrices(lc)

    tm_in = min(512, n)
    tm_c = min(256, lc)
    tc = math.gcd(64, math.gcd(n, lc))

    xc = ctx
    for l in range(depth):
        need_ctx = l < depth - 1
        mods = [mod_all[l, :b, i][:, None, :] for i in range(N_MOD)]
        mods_c = [jnp.broadcast_to(mod_all[l, b, i][None, None, :], (b, 1, d)) for i in range(N_MOD)]
        vec = lambda t: t[l].reshape(1, -1).astype(F32)
        w_in_l = w_in[l].astype(BF16)
        gq = jnp.tile(g_q_attn[l].astype(F32), ATT_HEADS)[None, :]
        gk = jnp.tile(g_k_attn[l].astype(F32), ATT_KV_HEADS)[None, :]

        qa, ka, va, us, nq, nk, nv, fn = _in_proj(x, mods[0], mods[1], vec(g_pre_mix), w_in_l, gq, gk, hm,
                                                  rope_tabs, tm_in)
        qac, kac, vac, usc, nqc, nkc, nvc, fnc = _in_proj(xc, mods_c[0], mods_c[1], vec(g_pre_mix), w_in_l,
                                                          gq, gk, hm, None, tm_c)

        oa = _gqa(qa, ka, va, kac, vac, group=ATT_HEADS // ATT_KV_HEADS, tq=min(256, n))
        on = _na(nq, nk, nv, nkc, nvc, _na_bias_table(na_rel_bias[l], rows), rt=min(4, rows))
        w_fn = w_fnet[l].astype(BF16)
        od = _fnet(fn, dft_n, ccs_n, w_fn, vec(b_fnet), tk=min(512, n))

        bd, cd, a_row = _s5_matrices(abr[l], abi[l], bbr[l], bbi[l], s5_c_re[l], s5_c_im[l])
        obc, ob = _s5(usc, us, bd, cd, a_row, vec(s5_d), w_s5_glu[l].astype(BF16), tc)

        consts = (w_out[l].astype(BF16), vec(g_post_mix), vec(g_pre_mlp), vec(g_post_mlp),
                  w_mlp1[l].astype(BF16), w_mlp2[l].astype(BF16))
        x_new = _out_mlp(x, oa, ob, on, od, (mods[2], mods[3], mods[4], mods[5]), *consts, tm_in)
        if need_ctx:
            oac = _gqa(qac, kac, vac, group=ATT_HEADS // ATT_KV_HEADS, tq=tm_c)
            onc = _mha(nqc, nkc, nvc, tq=tm_c)
            odc = _fnet(fnc, dft_c, ccs_c, w_fn, vec(b_fnet), tk=tm_c)
            xc = _out_mlp(xc, oac, obc, onc, odc, (mods_c[2], mods_c[3], mods_c[4], mods_c[5]), *consts, tm_c)
        x = x_new
    return x
```

```python
import functools
import math

import jax
import jax.numpy as jnp
import numpy as np
from jax import lax
from jax.experimental import pallas as pl
from jax.experimental.pallas import tpu as pltpu

F32 = jnp.float32
BF16 = jnp.bfloat16
HIGHEST = lax.Precision.HIGHEST

HEAD_DIM = 64
GRID_W = 64
N_MOD = 6
EPS = 1e-6
ATT_HEADS = 4
ATT_KV_HEADS = 2
ROPE_BASE = 10000.0
S5_GROUPS = 16
S5_GROUP_CH = 16
S5_STATE = 64
S5_MIN_DECAY = 1e-4
NA_HEADS = 4
NA_ROWS = 8
NA_COLS = 16
FN_GROUPS = 4
GW = 256
MASK_VALUE = -1e30
LOG2E = 1.4426950408889634
KEY_CHUNK = 1024

OFF_ATT_Q, OFF_ATT_K, OFF_ATT_V, OFF_S5 = 0, 256, 384, 512
OFF_NA_Q, OFF_NA_K, OFF_NA_V, OFF_FN, IN_WIDTH = 768, 1024, 1280, 1536, 1792

VMEM_LIMIT = 56 * 1024 * 1024


def _params(*sem):
    return pltpu.CompilerParams(dimension_semantics=sem, vmem_limit_bytes=VMEM_LIMIT)


def _const_spec(shape):
    return pl.BlockSpec(shape, lambda *_: (0,) * len(shape), pipeline_mode=pl.Buffered(1))


def _rms(x, g):
    return x * lax.rsqrt(jnp.mean(x * x, axis=-1, keepdims=True) + EPS) * g


def _dot2(a, b):
    half = a.shape[0] // 2
    return jnp.concatenate([jnp.dot(a[:half], b, preferred_element_type=F32),
                            jnp.dot(a[half:], b, preferred_element_type=F32)], axis=0)


def _nt_dot(a, b):
    return lax.dot_general(a, b, (((1,), (1,)), ((), ())), preferred_element_type=F32)


def _ada_body(c_ref, w_ref, b_ref, o_ref):
    c = c_ref[...]
    s = c * jax.nn.sigmoid(c)
    o_ref[...] = jnp.dot(s, w_ref[...], precision=HIGHEST, preferred_element_type=F32) + b_ref[...]


def _ada(cc, w_ada, b_ada, tn=1536):
    depth, d, n_out = w_ada.shape
    rows = cc.shape[0]
    return pl.pallas_call(
        _ada_body,
        grid=(depth, n_out // tn),
        in_specs=[
            pl.BlockSpec((rows, d), lambda l, j: (0, 0)),
            pl.BlockSpec((None, d, tn), lambda l, j: (l, 0, j)),
            pl.BlockSpec((None, 1, tn), lambda l, j: (l, 0, j)),
        ],
        out_specs=pl.BlockSpec((None, rows, tn), lambda l, j: (l, 0, j)),
        out_shape=jax.ShapeDtypeStruct((depth, rows, n_out), F32),
        compiler_params=_params("parallel", "parallel"),
        name="ada",
    )(cc, w_ada, b_ada.reshape(depth, 1, n_out))


def _s5_prep_body(are_ref, aim_ref, ldt_ref, are_b_ref, aim_b_ref, ldt_b_ref, bre_ref, bim_ref,
                  abr_ref, abi_ref, bbr_ref, bbi_ref):
    def zoh(a_re, a_im, log_dt):
        lr = jnp.minimum(a_re, -S5_MIN_DECAY)
        dt = jnp.exp(log_dt)
        mag = jnp.exp(lr * dt)
        ab_r = mag * jnp.cos(a_im * dt)
        ab_i = mag * jnp.sin(a_im * dt)
        return lr, ab_r, ab_i

    _, ab_r, ab_i = zoh(are_ref[...], aim_ref[...], ldt_ref[...])
    abr_ref[...] = ab_r
    abi_ref[...] = ab_i
    lr, ab_r, ab_i = zoh(are_b_ref[...], aim_b_ref[...], ldt_b_ref[...])
    li = aim_b_ref[...]
    nr, ni = ab_r - 1.0, ab_i
    den = lr * lr + li * li
    kr = (nr * lr + ni * li) / den
    ki = (ni * lr - nr * li) / den
    br, bi = bre_ref[...], bim_ref[...]
    bbr_ref[...] = kr * br - ki * bi
    bbi_ref[...] = kr * bi + ki * br


def _s5_prep(a_re, a_im, log_dt, b_re, b_im):
    lead = a_re.shape[:3]
    r = lead[0] * lead[1] * lead[2]
    p, h = S5_STATE, S5_GROUP_CH
    a2 = lambda t: t.reshape(r, p)
    ab = lambda t: jnp.broadcast_to(t.reshape(r, p, 1), (r, p, h)).reshape(r, p * h)
    ldt = jnp.broadcast_to(log_dt.reshape(r, 1), (r, p))
    ldt_b = jnp.broadcast_to(log_dt.reshape(r, 1), (r, p * h))
    small = jax.ShapeDtypeStruct((r, p), F32)
    big = jax.ShapeDtypeStruct((r, p * h), F32)
    abr, abi, bbr, bbi = pl.pallas_call(
        _s5_prep_body, out_shape=(small, small, big, big), name="s5_prep",
    )(a2(a_re), a2(a_im), ldt, ab(a_re), ab(a_im), ldt_b, b_re.reshape(r, p * h), b_im.reshape(r, p * h))
    return (abr.reshape(*lead, p), abi.reshape(*lead, p),
            bbr.reshape(*lead, p, h), bbi.reshape(*lead, p, h))


def _in_proj_body(*refs, rope):
    if rope:
        (x_ref, sh_ref, sc_ref, g_ref, w_ref, gq_ref, gk_ref, hm_ref, cos_ref, sa_ref, sb_ref,
         qa_ref, ka_ref, va_ref, us_ref, nq_ref, nk_ref, nv_ref, fn_ref) = refs
    else:
        (x_ref, sh_ref, sc_ref, g_ref, w_ref, gq_ref, gk_ref, hm_ref,
         qa_ref, ka_ref, va_ref, us_ref, nq_ref, nk_ref, nv_ref, fn_ref) = refs
    h = _rms(x_ref[...], g_ref[...]) * (1.0 + sc_ref[...]) + sh_ref[...]
    hb = h.astype(BF16)
    z_lo = jnp.dot(hb, w_ref[:, :OFF_NA_K], preferred_element_type=F32)
    z_hi = jnp.dot(hb, w_ref[:, OFF_NA_K:], preferred_element_type=F32)
    hm = hm_ref[...]

    def head_norm(t, g, avg):
        ms = jnp.dot((t * t).astype(BF16), avg, preferred_element_type=F32)
        return t * lax.rsqrt(ms + EPS) * g

    q = head_norm(z_lo[:, OFF_ATT_Q:OFF_ATT_K], gq_ref[...], hm)
    kw = OFF_ATT_V - OFF_ATT_K
    k = head_norm(z_lo[:, OFF_ATT_K:OFF_ATT_V], gk_ref[...], hm[:kw, :kw])
    if rope:
        def rot(t):
            w = t.shape[-1]
            half = HEAD_DIM // 4
            return (t * cos_ref[:, :w] + pltpu.roll(t, w - half, 1) * sa_ref[:, :w]
                    + pltpu.roll(t, half, 1) * sb_ref[:, :w])
        q, k = rot(q), rot(k)
    scale = HEAD_DIM ** -0.5 * LOG2E
    qa_ref[...] = (q * scale).astype(qa_ref.dtype)
    ka_ref[...] = k.astype(ka_ref.dtype)
    v = z_lo[:, OFF_ATT_V:OFF_S5]
    lane = lax.broadcasted_iota(jnp.int32, (v.shape[0], HEAD_DIM), 1)
    one_col = jnp.where(lane == 0, 1.0, 0.0)
    v_ext = jnp.concatenate(
        sum(([v[:, HEAD_DIM * i:HEAD_DIM * (i + 1)], one_col] for i in range(ATT_KV_HEADS)), []), axis=-1)
    va_ref[...] = v_ext.T.astype(va_ref.dtype)
    us_ref[...] = z_lo[:, OFF_S5:OFF_NA_Q].astype(us_ref.dtype)
    nq_ref[...] = (z_lo[:, OFF_NA_Q:OFF_NA_K] * scale).astype(nq_ref.dtype)
    nk_ref[...] = z_hi[:, :GW].astype(nk_ref.dtype)
    nv_ref[...] = z_hi[:, GW:2 * GW].astype(nv_ref.dtype)
    fn_ref[...] = z_hi[:, 2 * GW:].astype(fn_ref.dtype)


def _in_proj(x, sh, sc, g_pre, w_in, gq, gk, hm, rope_tabs, tm):
    b, n, d = x.shape
    rope = rope_tabs is not None
    row = lambda w: pl.BlockSpec((None, tm, w), lambda bi, i: (bi, i, 0))
    mod = pl.BlockSpec((None, 1, d), lambda bi, i: (bi, 0, 0))
    in_specs = [row(d), mod, mod, _const_spec((1, d)), _const_spec(w_in.shape),
                _const_spec(gq.shape), _const_spec(gk.shape), _const_spec(hm.shape)]
    args = [x, sh, sc, g_pre, w_in, gq, gk, hm]
    if rope:
        in_specs += [pl.BlockSpec((tm, GW), lambda bi, i: (i, 0))] * 3
        args += list(rope_tabs)
    kvw = ATT_KV_HEADS * HEAD_DIM
    widths = [GW, kvw, None, GW, GW, GW, GW, GW]
    out_specs = [pl.BlockSpec((None, 2 * kvw, tm), lambda bi, i: (bi, 0, i)) if w is None else row(w) for w in widths]
    out_shape = [jax.ShapeDtypeStruct((b, 2 * kvw, n) if w is None else (b, n, w), BF16) for w in widths]
    return pl.pallas_call(
        functools.partial(_in_proj_body, rope=rope),
        grid=(b, n // tm), in_specs=in_specs, out_specs=out_specs, out_shape=out_shape,
        compiler_params=_params("parallel", "parallel"), name="in_proj_rope" if rope else "in_proj",
    )(*args)


def _col_max(s):
    for rows in (256, 64):
        if s.shape[0] > rows and s.shape[0] % rows == 0:
            s = jnp.max(s.reshape(s.shape[0] // rows, rows, s.shape[1]), axis=0)
    return jnp.max(s, axis=0, keepdims=True)


def _gqa_body(*refs, n_heads, group, two_sets):
    if two_sets:
        q_ref, k1_ref, v1t_ref, k2_ref, v2t_ref, o_ref = refs
    else:
        q_ref, k1_ref, v1t_ref, o_ref = refs
    key_sets = [(k1_ref, v1t_ref)] + ([(k2_ref, v2t_ref)] if two_sets else [])

    def scores(h):
        ks = slice(HEAD_DIM * (h // group), HEAD_DIM * (h // group + 1))
        qh = q_ref[:, HEAD_DIM * h:HEAD_DIM * (h + 1)]
        s = [_nt_dot(k_ref[:, ks], qh).astype(BF16) for k_ref, _ in key_sets]
        m = functools.reduce(jnp.maximum, [_col_max(t) for t in s])
        return s, m

    def weighted_values(h, s, m):
        vr = slice(2 * HEAD_DIM * (h // group), 2 * HEAD_DIM * (h // group + 1))
        o = None
        for t, (_, vt_ref) in zip(s, key_sets):
            for c0 in range(0, t.shape[0], KEY_CHUNK):
                p = jnp.exp2(t[c0:c0 + KEY_CHUNK] - m)
                part = jnp.dot(vt_ref[vr, c0:c0 + KEY_CHUNK], p, preferred_element_type=F32)
                o = part if o is None else o + part
        return o[:HEAD_DIM] / o[HEAD_DIM:HEAD_DIM + 1]

    outs = []
    nxt = scores(0)
    for h in range(n_heads):
        cur, nxt = nxt, (scores(h + 1) if h + 1 < n_heads else None)
        outs.append(weighted_values(h, *cur))
    o_ref[...] = jnp.concatenate(outs, axis=0).T.astype(o_ref.dtype)


def _gqa(q, k1, v1t, k2=None, v2t=None, *, group, tq):
    b, nq, qw = q.shape
    two_sets = k2 is not None
    full = lambda t: pl.BlockSpec((None,) + t.shape[1:], lambda bi, i: (bi, 0, 0))
    args = [q, k1, v1t] + ([k2, v2t] if two_sets else [])
    in_specs = [pl.BlockSpec((None, tq, qw), lambda bi, i: (bi, i, 0))] + [full(t) for t in args[1:]]
    return pl.pallas_call(
        functools.partial(_gqa_body, n_heads=qw // HEAD_DIM, group=group, two_sets=two_sets),
        grid=(b, nq // tq), in_specs=in_specs,
        out_specs=pl.BlockSpec((None, tq, qw), lambda bi, i: (bi, i, 0)),
        out_shape=jax.ShapeDtypeStruct((b, nq, qw), BF16),
        compiler_params=_params("parallel", "parallel"), name="gqa2" if two_sets else "gqa1",
    )(*args)


def _mha_body(q_ref, k_ref, v_ref, o_ref, *, n_heads):
    for h in range(n_heads):
        hs = slice(HEAD_DIM * h, HEAD_DIM * (h + 1))
        s = _nt_dot(q_ref[:, hs], k_ref[:, hs])
        p = jnp.exp2(s - jnp.max(s, axis=-1, keepdims=True))
        o = jnp.dot(p.astype(BF16), v_ref[:, hs], preferred_element_type=F32)
        o_ref[:, hs] = (o / jnp.sum(p, axis=-1, keepdims=True)).astype(o_ref.dtype)


def _mha(q, k, v, *, tq):
    b, nq, qw = q.shape
    full = lambda t: pl.BlockSpec((None,) + t.shape[1:], lambda bi, i: (bi, 0, 0))
    return pl.pallas_call(
        functools.partial(_mha_body, n_heads=qw // HEAD_DIM),
        grid=(b, nq // tq),
        in_specs=[pl.BlockSpec((None, tq, qw), lambda bi, i: (bi, i, 0)), full(k), full(v)],
        out_specs=pl.BlockSpec((None, tq, qw), lambda bi, i: (bi, i, 0)),
        out_shape=jax.ShapeDtypeStruct((b, nq, qw), BF16),
        compiler_params=_params("parallel", "parallel"), name="mha",
    )(q, k, v)


def _na_body(q_ref, k_ref, v_ref, kc_ref, vc_ref, bias_ref, o_ref, *, rt, k_r, rows):
    i = pl.program_id(1)
    kc = kc_ref[...]
    vc = vc_ref[...]
    lane_head = lax.broadcasted_iota(jnp.int32, (GRID_W, GW), 1) // HEAD_DIM
    for j in range(rt):
        r = i * rt + j
        rs = jnp.clip(r - k_r // 2, 0, rows - k_r)
        start = pl.multiple_of(rs * GRID_W, GRID_W)
        kw = k_ref[pl.ds(start, k_r * GRID_W), :]
        vw = v_ref[pl.ds(start, k_r * GRID_W), :]
        q = q_ref[j * GRID_W:(j + 1) * GRID_W, :]
        q4 = jnp.concatenate([jnp.where(lane_head == h, q, jnp.zeros_like(q)) for h in range(NA_HEADS)], axis=0)
        s_loc = (_nt_dot(q4, kw) + bias_ref[r - rs]).astype(BF16)
        s_ctx = _nt_dot(q4, kc).astype(BF16)
        m = jnp.maximum(jnp.max(s_loc, axis=-1, keepdims=True), jnp.max(s_ctx, axis=-1, keepdims=True))
        p_loc = jnp.exp2(s_loc - m)
        p_ctx = jnp.exp2(s_ctx - m)
        l = (jnp.sum(p_loc.astype(F32), axis=-1, keepdims=True)
             + jnp.sum(p_ctx.astype(F32), axis=-1, keepdims=True))
        o4 = (jnp.dot(p_loc, vw, preferred_element_type=F32)
              + jnp.dot(p_ctx, vc, preferred_element_type=F32)) / l
        o = jnp.zeros((GRID_W, GW), F32)
        for h in range(NA_HEADS):
            o = o + jnp.where(lane_head == h, o4[h * GRID_W:(h + 1) * GRID_W, :], 0.0)
        o_ref[j * GRID_W:(j + 1) * GRID_W, :] = o.astype(o_ref.dtype)


def _na(q, k, v, kc, vc, bias, rt):
    b, n, _ = q.shape
    rows = n // GRID_W
    k_r = bias.shape[0]
    full = lambda t: pl.BlockSpec((None,) + t.shape[1:], lambda bi, i: (bi, 0, 0))
    return pl.pallas_call(
        functools.partial(_na_body, rt=rt, k_r=k_r, rows=rows),
        grid=(b, rows // rt),
        in_specs=[pl.BlockSpec((None, rt * GRID_W, GW), lambda bi, i: (bi, i, 0)),
                  full(k), full(v), full(kc), full(vc), _const_spec(bias.shape)],
        out_specs=pl.BlockSpec((None, rt * GRID_W, GW), lambda bi, i: (bi, i, 0)),
        out_shape=jax.ShapeDtypeStruct((b, n, GW), BF16),
        compiler_params=_params("parallel", "parallel"), name="na",
    )(q, k, v, kc, vc, bias)


def _na_bias_table(rel_bias, rows):
    k_r = min(NA_ROWS, rows)
    cols = np.arange(GRID_W)
    col_start = np.clip(cols - NA_COLS // 2, 0, GRID_W - NA_COLS)
    inside = (cols[None, :] >= col_start[:, None]) & (cols[None, :] < col_start[:, None] + NA_COLS)
    rel_c = cols[None, :] - cols[:, None] + (NA_COLS - 1)
    rel_r = np.arange(k_r)[None, :] - np.arange(k_r)[:, None] + (NA_ROWS - 1)
    pick_r = (rel_r[:, :, None] == np.arange(2 * NA_ROWS - 1)).astype(np.float32)
    pick_c = ((rel_c[:, :, None] == np.arange(2 * NA_COLS - 1)) & inside[:, :, None]).astype(np.float32)
    t = jnp.einsum("cai,him->cham", pick_r, rel_bias.astype(F32), precision=HIGHEST)
    t = jnp.einsum("cham,jkm->chjak", t, pick_c, precision=HIGHEST)
    t = t + np.where(inside, 0.0, MASK_VALUE).astype(np.float32)[None, None, :, None, :]
    return (t * LOG2E).reshape(k_r, NA_HEADS * GRID_W, k_r * GRID_W)


def _s5_scan(hs_ref, st_ref, a_ref, *, tc, nb, n_state, col_w, reverse):
    for c0 in range(0, n_state, col_w):
        re = slice(c0, c0 + col_w)
        im = slice(n_state + c0, n_state + c0 + col_w)
        ar = jnp.broadcast_to(a_ref[:, re], (nb, col_w))
        ai = jnp.broadcast_to(a_ref[:, im], (nb, col_w))

        hr, hi = st_ref[:, re], st_ref[:, im]
        for t in range(tc):
            rows = slice((tc - 1 - t if reverse else t) * nb, (tc - t if reverse else t + 1) * nb)
            hr, hi = ar * hr - ai * hi + hs_ref[rows, re], ar * hi + ai * hr + hs_ref[rows, im]
            hs_ref[rows, re] = hr
            hs_ref[rows, im] = hi
        st_ref[:, re] = hr
        st_ref[:, im] = hi


def _s5_fwd_body(uc_ref, ul_ref, perm_ref, bd_ref, cd_ref, a_ref, ut_ref, yf_ref, hs0_ref, hs1_ref, st_ref,
                 *, n_c, scan):
    j = pl.program_id(0)

    @pl.when(j == 0)
    def _():
        st_ref[...] = jnp.zeros_like(st_ref)
        hs1_ref[...] = jnp.zeros_like(hs1_ref)

    def step(cur, prev):
        nb, tc, w = uc_ref.shape
        u_bm = jnp.where(j < n_c, uc_ref[...], ul_ref[...]).reshape(nb * tc, w)
        ut = _dot2(perm_ref[...], u_bm).astype(ut_ref.dtype)
        ut_ref[...] = ut
        cur[...] = _dot2(ut, bd_ref[...])
        scan(prev, st_ref, a_ref, reverse=False)
        yf_ref[...] = _dot2(prev[...].astype(BF16), cd_ref[...])

    pl.when(j % 2 == 0)(lambda: step(hs0_ref, hs1_ref))
    pl.when(j % 2 == 1)(lambda: step(hs1_ref, hs0_ref))


def _s5_bwd_body(ut_ref, utp_ref, yf_ref, permt_ref, bd_ref, cd_ref, a_ref, dsk_ref, wglu_ref, oc_ref, ol_ref,
                 hs0_ref, hs1_ref, st_ref, *, n_c, scan):
    j = pl.program_id(0)

    @pl.when(j == 0)
    def _():
        st_ref[...] = jnp.zeros_like(st_ref)
        hs1_ref[...] = jnp.zeros_like(hs1_ref)

    def step(cur, prev):
        cur[...] = _dot2(ut_ref[...], bd_ref[...])
        scan(prev, st_ref, a_ref, reverse=True)
        t = (dsk_ref[...] * utp_ref[...].astype(F32) + yf_ref[...]
             + _dot2(prev[...].astype(BF16), cd_ref[...]))
        g = jax.nn.gelu(t)
        ob = g * jax.nn.sigmoid(_dot2(g.astype(BF16), wglu_ref[...]))
        ob = _dot2(permt_ref[...], ob.astype(BF16)).astype(BF16)

        @pl.when(j <= n_c)
        def _():
            oc_ref[...] = ob.reshape(oc_ref.shape)

        @pl.when(j > n_c)
        def _():
            ol_ref[...] = ob.reshape(ol_ref.shape)

    pl.when(j % 2 == 0)(lambda: step(hs0_ref, hs1_ref))
    pl.when(j % 2 == 1)(lambda: step(hs1_ref, hs0_ref))


def _s5(usc, us, bd, cd, a_row, dsk, wglu, tc):
    nb, lc, _ = usc.shape
    n = us.shape[1]
    n_c, n_l = lc // tc, n // tc
    n_all = n_c + n_l
    n_state2 = bd.shape[-1]
    rows = tc * nb
    r = np.arange(rows)
    perm = np.zeros((rows, rows), np.float32)
    perm[r, (r % nb) * tc + r // nb] = 1.0
    scan = functools.partial(_s5_scan, tc=tc, nb=nb, n_state=n_state2 // 2, col_w=512)
    scratch = [pltpu.VMEM((rows, n_state2), F32), pltpu.VMEM((nb, n_state2), F32)]
    seg = lambda idx: pl.BlockSpec((nb, tc, GW), lambda j: (0, idx(j), 0))
    tm_rows = lambda idx: pl.BlockSpec((rows, GW), lambda j: (idx(j), 0))

    ut, yf = pl.pallas_call(
        functools.partial(_s5_fwd_body, n_c=n_c, scan=scan), grid=(n_all + 1,),
        in_specs=[seg(lambda j: jnp.minimum(j, n_c - 1)), seg(lambda j: jnp.clip(j - n_c, 0, n_l - 1)),
                  _const_spec(perm.shape), _const_spec(bd.shape[1:]), _const_spec(cd.shape[1:]),
                  _const_spec(a_row.shape[1:])],
        out_specs=[tm_rows(lambda j: jnp.minimum(j, n_all - 1)), tm_rows(lambda j: jnp.maximum(j - 1, 0))],
        out_shape=[jax.ShapeDtypeStruct((n_all * rows, GW), BF16), jax.ShapeDtypeStruct((n_all * rows, GW), F32)],
        scratch_shapes=[scratch[0], scratch[0], scratch[1]],
        compiler_params=_params("arbitrary"), name="s5_fwd",
    )(usc, us, jnp.asarray(perm, BF16), bd[0], cd[0], a_row[0])

    def blk(j):
        j = jnp.clip(j, 0, n_all - 1)
        return jnp.where(j < n_c, n_c - 1 - j, n_c + n_all - 1 - j)

    prev = lambda j: jnp.maximum(j - 1, 0)
    return pl.pallas_call(
        functools.partial(_s5_bwd_body, n_c=n_c, scan=scan), grid=(n_all + 1,),
        in_specs=[tm_rows(blk), tm_rows(lambda j: blk(prev(j))), tm_rows(lambda j: blk(prev(j))),
                  _const_spec(perm.shape), _const_spec(bd.shape[1:]), _const_spec(cd.shape[1:]),
                  _const_spec(a_row.shape[1:]), _const_spec(dsk.shape), _const_spec(wglu.shape)],
        out_specs=[seg(lambda j: jnp.maximum(n_c - 1 - prev(j), 0)),
                   seg(lambda j: n_l - 1 - jnp.maximum(prev(j) - n_c, 0))],
        out_shape=[jax.ShapeDtypeStruct(usc.shape, BF16), jax.ShapeDtypeStruct(us.shape, BF16)],
        scratch_shapes=[scratch[0], scratch[0], scratch[1]], compiler_params=_params("arbitrary"), name="s5_bwd",
    )(ut, ut, yf, jnp.asarray(perm.T, BF16), bd[1], cd[1], a_row[1], dsk, wglu)


def _s5_matrices(abr, abi, bbr, bbi, c_re, c_im):
    g, p, h = S5_GROUPS, S5_STATE, S5_GROUP_CH
    eye = jnp.eye(g, dtype=F32)
    to_bd = lambda t: jnp.einsum("dgph,gk->dghkp", t, eye).reshape(2, g * h, g * p)
    bd = jnp.concatenate([to_bd(bbr), to_bd(bbi)], axis=-1)
    to_cd = lambda t: jnp.einsum("dghp,gk->dgpkh", t, eye).reshape(2, g * p, g * h)
    cd = jnp.concatenate([to_cd(c_re.astype(F32)), to_cd(-c_im.astype(F32))], axis=1)
    a_row = jnp.concatenate([abr.reshape(2, 1, g * p), abi.reshape(2, 1, g * p)], axis=-1)
    return bd.astype(BF16), cd.astype(BF16), a_row


def _fnet_body(u_ref, dft_ref, ccs_ref, w_ref, b_ref, o_ref, uc_ref):
    n = u_ref.shape[0]

    @pl.when(pl.program_id(1) == 0)
    def _():
        t = jnp.dot(u_ref[...], ccs_ref[...], preferred_element_type=F32)
        uc_ref[0:n, :] = t[:, :GW].astype(BF16)
        uc_ref[n:2 * n, :] = t[:, GW:].astype(BF16)

    tk = o_ref.shape[0]
    rows = pl.ds(pl.multiple_of(pl.program_id(1) * tk, tk), tk)
    f = _dot2(dft_ref[rows, :], uc_ref[...])
    o = jnp.dot(f.astype(BF16), w_ref[...], preferred_element_type=F32) + b_ref[...]
    o_ref[...] = o.astype(o_ref.dtype)


def _fnet(u, dft, ccs, w, bias, tk):
    b, n, _ = u.shape
    return pl.pallas_call(
        _fnet_body, grid=(b, n // tk),
        in_specs=[pl.BlockSpec((None, n, GW), lambda bi, k: (bi, 0, 0)), _const_spec(dft.shape),
                  _const_spec(ccs.shape), _const_spec(w.shape), _const_spec(bias.shape)],
        out_specs=pl.BlockSpec((None, tk, GW), lambda bi, k: (bi, k, 0)),
        out_shape=jax.ShapeDtypeStruct((b, n, GW), BF16),
        scratch_shapes=[pltpu.VMEM((2 * n, GW), BF16)],
        compiler_params=_params("parallel", "arbitrary"), name="fnet",
    )(u, dft, ccs, w, bias)


def _dft_matrices(n):
    ch = GW // FN_GROUPS
    lo = 64
    k = jnp.arange(n)[:, None]
    ang_a = (2.0 * math.pi * lo / n) * ((k * jnp.arange(n // lo)[None, :]) % (n // lo)).astype(F32)
    ang_b = (2.0 * math.pi / n) * ((k * jnp.arange(lo)[None, :]) % n).astype(F32)
    ca, sa = jnp.cos(ang_a)[:, :, None], jnp.sin(ang_a)[:, :, None]
    cb, sb = jnp.cos(ang_b)[:, None, :], jnp.sin(ang_b)[:, None, :]
    dft = jnp.stack([ca * cb - sa * sb, -(sa * cb + ca * sb)], axis=1).reshape(n, 2 * n).astype(BF16)
    c = jnp.arange(GW)
    same = (c[:, None] // ch) == (c[None, :] // ch)
    angc = (2.0 * math.pi / ch) * (((c[:, None] % ch) * (c[None, :] % ch)) % ch).astype(F32)
    norm = 1.0 / math.sqrt(n * ch)
    cc = jnp.where(same, jnp.cos(angc), 0.0) * norm
    sc = jnp.where(same, jnp.sin(angc), 0.0) * norm
    return dft, jnp.concatenate([cc, sc], axis=1).astype(BF16)


def _out_mlp_body(x_ref, oa_ref, ob_ref, on_ref, od_ref, g1_ref, sh2_ref, sc2_ref, g2_ref,
                  wout_ref, gpm_ref, gpre_ref, gpost_ref, w1_ref, w2_ref, o_ref, *, ff_chunk):
    cat = jnp.concatenate([oa_ref[...], ob_ref[...], on_ref[...], od_ref[...]], axis=-1)
    y = jnp.dot(cat, wout_ref[...], preferred_element_type=F32)
    x1 = x_ref[...] + g1_ref[...] * _rms(y, gpm_ref[...])
    h = (_rms(x1, gpre_ref[...]) * (1.0 + sc2_ref[...]) + sh2_ref[...]).astype(BF16)
    d_ff = w1_ref.shape[1]
    m = jnp.zeros(x1.shape, F32)
    for c0 in range(0, d_ff, ff_chunk):
        a = jnp.maximum(jnp.dot(h, w1_ref[:, c0:c0 + ff_chunk], preferred_element_type=F32), 0.0)
        m = m + jnp.dot((a * a).astype(BF16), w2_ref[c0:c0 + ff_chunk, :], preferred_element_type=F32)
    o_ref[...] = x1 + g2_ref[...] * _rms(m, gpost_ref[...])


def _out_mlp(x, oa, ob, on, od, mods, wout, gpm, gpre, gpost, w1, w2, tm):
    b, n, d = x.shape
    row = lambda w: pl.BlockSpec((None, tm, w), lambda bi, i: (bi, i, 0))
    mod = pl.BlockSpec((None, 1, d), lambda bi, i: (bi, 0, 0))
    consts = [wout, gpm, gpre, gpost, w1, w2]
    return pl.pallas_call(
        functools.partial(_out_mlp_body, ff_chunk=1024),
        grid=(b, n // tm),
        in_specs=[row(d), row(GW), row(GW), row(GW), row(GW), mod, mod, mod, mod]
        + [_const_spec(t.shape) for t in consts],
        out_specs=row(d), out_shape=jax.ShapeDtypeStruct((b, n, d), F32),
        compiler_params=_params("parallel", "parallel"), name="out_mlp",
    )(x, oa, ob, on, od, *mods, *consts)


def _rope_tables(n):
    half = HEAD_DIM // 4
    pos = jnp.arange(n, dtype=jnp.int32)
    row = (pos // GRID_W).astype(F32)
    col = (pos % GRID_W).astype(F32)
    lane = jnp.arange(GW)
    jj = lane % HEAD_DIM
    inv = ROPE_BASE ** (-(jj % half).astype(F32) / half)
    p = jnp.where((jj // (2 * half) == 0)[None, :], row[:, None], col[:, None])
    ang = p * inv[None, :]
    first = ((jj % (2 * half)) < half)[None, :]
    sin = jnp.sin(ang)
    return jnp.cos(ang), jnp.where(first, -sin, 0.0), jnp.where(first, 0.0, sin)


def kernel(x, c, ctx, c_ctx, w_ada, b_ada, g_pre_mix, g_post_mix, g_pre_mlp, g_post_mlp, w_in, g_q_attn, g_k_attn,
           s5_a_re, s5_a_im, s5_log_dt, s5_b_re, s5_b_im, s5_c_re, s5_c_im, s5_d, w_s5_glu, na_rel_bias,
           w_fnet, b_fnet, w_out, w_mlp1, w_mlp2):
    b, n, d = x.shape
    lc = ctx.shape[1]
    depth = w_ada.shape[0]
    rows = n // GRID_W

    n_rows = -(-(b + 1) // 8) * 8
    cc = jnp.concatenate([c, c_ctx[None, :], jnp.zeros((n_rows - b - 1, d), c.dtype)], axis=0)
    mod_all = _ada(cc, w_ada, b_ada).reshape(depth, n_rows, N_MOD, d)

    abr, abi, bbr, bbi = _s5_prep(s5_a_re, s5_a_im, s5_log_dt, s5_b_re, s5_b_im)
    rope_tabs = _rope_tables(n)
    lane = jnp.arange(GW)
    hm = jnp.where((lane[:, None] // HEAD_DIM) == (lane[None, :] // HEAD_DIM), 1.0 / HEAD_DIM, 0.0).astype(BF16)
    dft_n, ccs_n = _dft_matrices(n)
    dft_c, ccs_c = _dft_matrices(lc)

    tm_in = min(512, n)
    tm_c = min(256, lc)
    tc = math.gcd(64, math.gcd(n, lc))

    xc = ctx
    for l in range(depth):
        need_ctx = l < depth - 1
        mods = [mod_all[l, :b, i][:, None, :] for i in range(N_MOD)]
        mods_c = [jnp.broadcast_to(mod_all[l, b, i][None, None, :], (b, 1, d)) for i in range(N_MOD)]
        vec = lambda t: t[l].reshape(1, -1).astype(F32)
        w_in_l = w_in[l].astype(BF16)
        gq = jnp.tile(g_q_attn[l].astype(F32), ATT_HEADS)[None, :]
        gk = jnp.tile(g_k_attn[l].astype(F32), ATT_KV_HEADS)[None, :]

        qa, ka, va, us, nq, nk, nv, fn = _in_proj(x, mods[0], mods[1], vec(g_pre_mix), w_in_l, gq, gk, hm,
                                                  rope_tabs, tm_in)
        qac, kac, vac, usc, nqc, nkc, nvc, fnc = _in_proj(xc, mods_c[0], mods_c[1], vec(g_pre_mix), w_in_l,
                                                          gq, gk, hm, None, tm_c)

        oa = _gqa(qa, ka, va, kac, vac, group=ATT_HEADS // ATT_KV_HEADS, tq=min(256, n))
        on = _na(nq, nk, nv, nkc, nvc, _na_bias_table(na_rel_bias[l], rows), rt=min(4, rows))
        w_fn = w_fnet[l].astype(BF16)
        od = _fnet(fn, dft_n, ccs_n, w_fn, vec(b_fnet), tk=min(512, n))

        bd, cd, a_row = _s5_matrices(abr[l], abi[l], bbr[l], bbi[l], s5_c_re[l], s5_c_im[l])
        obc, ob = _s5(usc, us, bd, cd, a_row, vec(s5_d), w_s5_glu[l].astype(BF16), tc)

        consts = (w_out[l].astype(BF16), vec(g_post_mix), vec(g_pre_mlp), vec(g_post_mlp),
                  w_mlp1[l].astype(BF16), w_mlp2[l].astype(BF16))
        x_new = _out_mlp(x, oa, ob, on, od, (mods[2], mods[3], mods[4], mods[5]), *consts, tm_in)
        if need_ctx:
            oac = _gqa(qac, kac, vac, group=ATT_HEADS // ATT_KV_HEADS, tq=tm_c)
            onc = _mha(nqc, nkc, nvc, tq=tm_c)
            odc = _fnet(fnc, dft_c, ccs_c, w_fn, vec(b_fnet), tk=tm_c)
            xc = _out_mlp(xc, oac, obc, onc, odc, (mods_c[2], mods_c[3], mods_c[4], mods_c[5]), *consts, tm_c)
        x = x_new
    return x
```

```python
import functools
import math

import jax
import jax.numpy as jnp
import numpy as np
from jax import lax
from jax.experimental import pallas as pl
from jax.experimental.pallas import tpu as pltpu

F32 = jnp.float32
BF16 = jnp.bfloat16
HIGHEST = lax.Precision.HIGHEST

HEAD_DIM = 64
GRID_W = 64
N_MOD = 6
EPS = 1e-6
ATT_HEADS = 4
ATT_KV_HEADS = 2
ROPE_BASE = 10000.0
S5_GROUPS = 16
S5_GROUP_CH = 16
S5_STATE = 64
S5_MIN_DECAY = 1e-4
NA_HEADS = 4
NA_ROWS = 8
NA_COLS = 16
FN_GROUPS = 4
GW = 256
MASK_VALUE = -1e30
LOG2E = 1.4426950408889634
KEY_CHUNK = 1024

OFF_ATT_Q, OFF_ATT_K, OFF_ATT_V, OFF_S5 = 0, 256, 384, 512
OFF_NA_Q, OFF_NA_K, OFF_NA_V, OFF_FN, IN_WIDTH = 768, 1024, 1280, 1536, 1792

VMEM_LIMIT = 56 * 1024 * 1024


def _params(*sem):
    return pltpu.CompilerParams(dimension_semantics=sem, vmem_limit_bytes=VMEM_LIMIT)


def _const_spec(shape):
    return pl.BlockSpec(shape, lambda *_: (0,) * len(shape), pipeline_mode=pl.Buffered(1))


def _layer_spec(arr, *lead):
    k = len(lead)
    return pl.BlockSpec((None,) * k + arr.shape[k:], lambda *_: tuple(lead) + (0,) * (arr.ndim - k),
                        pipeline_mode=pl.Buffered(1))


def _rms(x, g):
    return x * lax.rsqrt(jnp.mean(x * x, axis=-1, keepdims=True) + EPS) * g


def _dot2(a, b):
    half = a.shape[0] // 2
    return jnp.concatenate([jnp.dot(a[:half], b, preferred_element_type=F32),
                            jnp.dot(a[half:], b, preferred_element_type=F32)], axis=0)


def _nt_dot(a, b):
    return lax.dot_general(a, b, (((1,), (1,)), ((), ())), preferred_element_type=F32)


def _ada_body(c_ref, w_ref, b_ref, o_ref):
    c = c_ref[...]
    s = c * jax.nn.sigmoid(c)
    o_ref[...] = jnp.dot(s, w_ref[...], precision=HIGHEST, preferred_element_type=F32) + b_ref[...]


def _ada(cc, w_ada, b_ada):
    depth, d, n_out = w_ada.shape
    rows = cc.shape[0]
    out = pl.pallas_call(
        _ada_body,
        grid=(depth, n_out // d),
        in_specs=[
            pl.BlockSpec((rows, d), lambda l, j: (0, 0)),
            pl.BlockSpec((None, d, d), lambda l, j: (l, 0, j)),
            pl.BlockSpec((None, 1, d), lambda l, j: (l, 0, j)),
        ],
        out_specs=pl.BlockSpec((None, None, rows, d), lambda l, j: (l, j, 0, 0)),
        out_shape=jax.ShapeDtypeStruct((depth, n_out // d, rows, d), F32),
        compiler_params=_params("parallel", "parallel"),
        name="ada",
    )(cc, w_ada, b_ada.reshape(depth, 1, n_out))
    return out.reshape(depth, n_out // d, rows, 1, d)


def _mod_spec(mod_t, layer, which, row):
    d = mod_t.shape[-1]
    return pl.BlockSpec((None, None, None, 1, d),
                        lambda bi, i: (layer, which, bi if row is None else row, 0, 0))


def _s5_prep_body(are_ref, aim_ref, ldt_ref, are_b_ref, aim_b_ref, ldt_b_ref, bre_ref, bim_ref,
                  abr_ref, abi_ref, bbr_ref, bbi_ref):
    def zoh(a_re, a_im, log_dt):
        lr = jnp.minimum(a_re, -S5_MIN_DECAY)
        dt = jnp.exp(log_dt)
        mag = jnp.exp(lr * dt)
        ab_r = mag * jnp.cos(a_im * dt)
        ab_i = mag * jnp.sin(a_im * dt)
        return lr, ab_r, ab_i

    _, ab_r, ab_i = zoh(are_ref[...], aim_ref[...], ldt_ref[...])
    abr_ref[...] = ab_r
    abi_ref[...] = ab_i
    lr, ab_r, ab_i = zoh(are_b_ref[...], aim_b_ref[...], ldt_b_ref[...])
    li = aim_b_ref[...]
    nr, ni = ab_r - 1.0, ab_i
    den = lr * lr + li * li
    kr = (nr * lr + ni * li) / den
    ki = (ni * lr - nr * li) / den
    br, bi = bre_ref[...], bim_ref[...]
    bbr_ref[...] = kr * br - ki * bi
    bbi_ref[...] = kr * bi + ki * br


def _s5_prep(a_re, a_im, log_dt, b_re, b_im):
    lead = a_re.shape[:3]
    r = lead[0] * lead[1] * lead[2]
    p, h = S5_STATE, S5_GROUP_CH
    a2 = lambda t: t.reshape(r, p)
    ab = lambda t: jnp.broadcast_to(t.reshape(r, p, 1), (r, p, h)).reshape(r, p * h)
    ldt = jnp.broadcast_to(log_dt.reshape(r, 1), (r, p))
    ldt_b = jnp.broadcast_to(log_dt.reshape(r, 1), (r, p * h))
    small = jax.ShapeDtypeStruct((r, p), F32)
    big = jax.ShapeDtypeStruct((r, p * h), F32)
    abr, abi, bbr, bbi = pl.pallas_call(
        _s5_prep_body, out_shape=(small, small, big, big), name="s5_prep",
    )(a2(a_re), a2(a_im), ldt, ab(a_re), ab(a_im), ldt_b, b_re.reshape(r, p * h), b_im.reshape(r, p * h))
    return (abr.reshape(*lead, p), abi.reshape(*lead, p),
            bbr.reshape(*lead, p, h), bbi.reshape(*lead, p, h))


def _in_proj_body(*refs, rope, sub):
    if rope:
        (x_ref, sh_ref, sc_ref, g_ref, w_ref, gq_ref, gk_ref, hm_ref, cos_ref, sa_ref, sb_ref,
         qa_ref, ka_ref, va_ref, us_ref, nq_ref, nk_ref, nv_ref, fn_ref) = refs
    else:
        (x_ref, sh_ref, sc_ref, g_ref, w_ref, gq_ref, gk_ref, hm_ref,
         qa_ref, ka_ref, va_ref, us_ref, nq_ref, nk_ref, nv_ref, fn_ref) = refs
    hm = hm_ref[...]
    kw = OFF_ATT_V - OFF_ATT_K
    scale = HEAD_DIM ** -0.5 * LOG2E
    lane = lax.broadcasted_iota(jnp.int32, (sub, HEAD_DIM), 1)
    one_col = jnp.where(lane == 0, 1.0, 0.0)

    def head_norm(t, g, avg):
        ms = jnp.dot((t * t).astype(BF16), avg, preferred_element_type=F32)
        return t * lax.rsqrt(ms + EPS) * g

    for r0 in range(0, x_ref.shape[0], sub):
        rs = slice(r0, r0 + sub)
        h = _rms(x_ref[rs, :], g_ref[...]) * (1.0 + sc_ref[...]) + sh_ref[...]
        hb = h.astype(BF16)
        z_lo = jnp.dot(hb, w_ref[:, :OFF_NA_K], preferred_element_type=F32)
        z_hi = jnp.dot(hb, w_ref[:, OFF_NA_K:], preferred_element_type=F32)
        q = head_norm(z_lo[:, OFF_ATT_Q:OFF_ATT_K], gq_ref[...], hm)
        k = head_norm(z_lo[:, OFF_ATT_K:OFF_ATT_V], gk_ref[...], hm[:kw, :kw])
        if rope:
            def rot(t):
                w = t.shape[-1]
                half = HEAD_DIM // 4
                return (t * cos_ref[rs, :w] + pltpu.roll(t, w - half, 1) * sa_ref[rs, :w]
                        + pltpu.roll(t, half, 1) * sb_ref[rs, :w])
            q, k = rot(q), rot(k)
        qa_ref[rs, :] = (q * scale).astype(qa_ref.dtype)
        ka_ref[rs, :] = k.astype(ka_ref.dtype)
        v = z_lo[:, OFF_ATT_V:OFF_S5]
        v_ext = jnp.concatenate(
            sum(([v[:, HEAD_DIM * i:HEAD_DIM * (i + 1)], one_col] for i in range(ATT_KV_HEADS)), []), axis=-1)
        va_ref[:, rs] = v_ext.T.astype(va_ref.dtype)
        us_ref[rs, :] = z_lo[:, OFF_S5:OFF_NA_Q].astype(us_ref.dtype)
        nq_ref[rs, :] = (z_lo[:, OFF_NA_Q:OFF_NA_K] * scale).astype(nq_ref.dtype)
        nk_ref[rs, :] = z_hi[:, :GW].astype(nk_ref.dtype)
        nv_ref[rs, :] = z_hi[:, GW:2 * GW].astype(nv_ref.dtype)
        fn_ref[rs, :] = z_hi[:, 2 * GW:].astype(fn_ref.dtype)


def _in_proj(x, mod_t, layer, mod_row, g_pre, w_in, gq, gk, hm, rope_tabs, tm):
    b, n, d = x.shape
    rope = rope_tabs is not None
    row = lambda w: pl.BlockSpec((None, tm, w), lambda bi, i: (bi, i, 0))
    in_specs = [row(d), _mod_spec(mod_t, layer, 0, mod_row), _mod_spec(mod_t, layer, 1, mod_row)]
    in_specs += [_layer_spec(t, layer) for t in (g_pre, w_in, gq, gk)] + [_const_spec(hm.shape)]
    args = [x, mod_t, mod_t, g_pre, w_in, gq, gk, hm]
    if rope:
        in_specs += [pl.BlockSpec((tm, GW), lambda bi, i: (i, 0))] * 3
        args += list(rope_tabs)
    kvw = ATT_KV_HEADS * HEAD_DIM
    widths = [GW, kvw, None, GW, GW, GW, GW, GW]
    out_specs = [pl.BlockSpec((None, 2 * kvw, tm), lambda bi, i: (bi, 0, i)) if w is None else row(w) for w in widths]
    out_shape = [jax.ShapeDtypeStruct((b, 2 * kvw, n) if w is None else (b, n, w), BF16) for w in widths]
    return pl.pallas_call(
        functools.partial(_in_proj_body, rope=rope, sub=min(256, tm)),
        grid=(b, n // tm), in_specs=in_specs, out_specs=out_specs, out_shape=out_shape,
        compiler_params=_params("parallel", "parallel"), name="in_proj_rope" if rope else "in_proj",
    )(*args)


def _col_max(s):
    for rows in (256, 64):
        if s.shape[0] > rows and s.shape[0] % rows == 0:
            s = jnp.max(s.reshape(s.shape[0] // rows, rows, s.shape[1]), axis=0)
    return jnp.max(s, axis=0, keepdims=True)


def _gqa_body(*refs, n_heads, group, two_sets):
    if two_sets:
        q_ref, k1_ref, v1t_ref, k2_ref, v2t_ref, o_ref = refs
    else:
        q_ref, k1_ref, v1t_ref, o_ref = refs
    key_sets = [(k1_ref, v1t_ref)] + ([(k2_ref, v2t_ref)] if two_sets else [])

    def scores(h):
        ks = slice(HEAD_DIM * (h // group), HEAD_DIM * (h // group + 1))
        qh = q_ref[:, HEAD_DIM * h:HEAD_DIM * (h + 1)]
        s = [_nt_dot(k_ref[:, ks], qh).astype(BF16) for k_ref, _ in key_sets]
        m = functools.reduce(jnp.maximum, [_col_max(t) for t in s])
        return s, m

    def weighted_values(h, s, m):
        vr = slice(2 * HEAD_DIM * (h // group), 2 * HEAD_DIM * (h // group + 1))
        o = None
        for t, (_, vt_ref) in zip(s, key_sets):
            for c0 in range(0, t.shape[0], KEY_CHUNK):
                p = jnp.exp2(t[c0:c0 + KEY_CHUNK] - m)
                part = jnp.dot(vt_ref[vr, c0:c0 + KEY_CHUNK], p, preferred_element_type=F32)
                o = part if o is None else o + part
        return o[:HEAD_DIM] / o[HEAD_DIM:HEAD_DIM + 1]

    outs = []
    nxt = scores(0)
    for h in range(n_heads):
        cur, nxt = nxt, (scores(h + 1) if h + 1 < n_heads else None)
        outs.append(weighted_values(h, *cur))
    o_ref[...] = jnp.concatenate(outs, axis=0).T.astype(o_ref.dtype)


def _gqa(q, k1, v1t, k2=None, v2t=None, *, group, tq):
    b, nq, qw = q.shape
    two_sets = k2 is not None
    full = lambda t: pl.BlockSpec((None,) + t.shape[1:], lambda bi, i: (bi, 0, 0))
    args = [q, k1, v1t] + ([k2, v2t] if two_sets else [])
    in_specs = [pl.BlockSpec((None, tq, qw), lambda bi, i: (bi, i, 0))] + [full(t) for t in args[1:]]
    return pl.pallas_call(
        functools.partial(_gqa_body, n_heads=qw // HEAD_DIM, group=group, two_sets=two_sets),
        grid=(b, nq // tq), in_specs=in_specs,
        out_specs=pl.BlockSpec((None, tq, qw), lambda bi, i: (bi, i, 0)),
        out_shape=jax.ShapeDtypeStruct((b, nq, qw), BF16),
        compiler_params=_params("parallel", "parallel"), name="gqa2" if two_sets else "gqa1",
    )(*args)


def _mha_body(q_ref, k_ref, v_ref, o_ref, *, n_heads):
    for h in range(n_heads):
        hs = slice(HEAD_DIM * h, HEAD_DIM * (h + 1))
        s = _nt_dot(q_ref[:, hs], k_ref[:, hs])
        p = jnp.exp2(s - jnp.max(s, axis=-1, keepdims=True))
        o = jnp.dot(p.astype(BF16), v_ref[:, hs], preferred_element_type=F32)
        o_ref[:, hs] = (o / jnp.sum(p, axis=-1, keepdims=True)).astype(o_ref.dtype)


def _mha(q, k, v, *, tq):
    b, nq, qw = q.shape
    full = lambda t: pl.BlockSpec((None,) + t.shape[1:], lambda bi, i: (bi, 0, 0))
    return pl.pallas_call(
        functools.partial(_mha_body, n_heads=qw // HEAD_DIM),
        grid=(b, nq // tq),
        in_specs=[pl.BlockSpec((None, tq, qw), lambda bi, i: (bi, i, 0)), full(k), full(v)],
        out_specs=pl.BlockSpec((None, tq, qw), lambda bi, i: (bi, i, 0)),
        out_shape=jax.ShapeDtypeStruct((b, nq, qw), BF16),
        compiler_params=_params("parallel", "parallel"), name="mha",
    )(q, k, v)


def _na_body(q_ref, k_ref, v_ref, kc_ref, vc_ref, bias_ref, o_ref, *, rt, k_r, rows):
    i = pl.program_id(1)
    kc = kc_ref[...]
    vc = vc_ref[...]
    lane_head = lax.broadcasted_iota(jnp.int32, (GRID_W, GW), 1) // HEAD_DIM
    def scores(j):
        r = i * rt + j
        rs = jnp.clip(r - k_r // 2, 0, rows - k_r)
        start = pl.multiple_of(rs * GRID_W, GRID_W)
        q = q_ref[j * GRID_W:(j + 1) * GRID_W, :]
        q4 = jnp.concatenate([jnp.where(lane_head == h, q, jnp.zeros_like(q)) for h in range(NA_HEADS)], axis=0)
        s_loc = (_nt_dot(q4, k_ref[pl.ds(start, k_r * GRID_W), :]) + bias_ref[r - rs]).astype(BF16)
        s_ctx = _nt_dot(q4, kc).astype(BF16)
        return start, s_loc, s_ctx

    def weighted_values(j, start, s_loc, s_ctx):
        m = jnp.maximum(jnp.max(s_loc, axis=-1, keepdims=True), jnp.max(s_ctx, axis=-1, keepdims=True))
        p_loc = jnp.exp2(s_loc - m)
        p_ctx = jnp.exp2(s_ctx - m)
        l = (jnp.sum(p_loc.astype(F32), axis=-1, keepdims=True)
             + jnp.sum(p_ctx.astype(F32), axis=-1, keepdims=True))
        o4 = (jnp.dot(p_loc, v_ref[pl.ds(start, k_r * GRID_W), :], preferred_element_type=F32)
              + jnp.dot(p_ctx, vc, preferred_element_type=F32)) / l
        o = jnp.zeros((GRID_W, GW), F32)
        for h in range(NA_HEADS):
            o = o + jnp.where(lane_head == h, o4[h * GRID_W:(h + 1) * GRID_W, :], 0.0)
        o_ref[j * GRID_W:(j + 1) * GRID_W, :] = o.astype(o_ref.dtype)

    nxt = scores(0)
    for j in range(rt):
        cur, nxt = nxt, (scores(j + 1) if j + 1 < rt else None)
        weighted_values(j, *cur)


def _na(q, k, v, kc, vc, bias, layer, rt):
    b, n, _ = q.shape
    rows = n // GRID_W
    k_r = bias.shape[1]
    full = lambda t: pl.BlockSpec((None,) + t.shape[1:], lambda bi, i: (bi, 0, 0))
    return pl.pallas_call(
        functools.partial(_na_body, rt=rt, k_r=k_r, rows=rows),
        grid=(b, rows // rt),
        in_specs=[pl.BlockSpec((None, rt * GRID_W, GW), lambda bi, i: (bi, i, 0)),
                  full(k), full(v), full(kc), full(vc), _layer_spec(bias, layer)],
        out_specs=pl.BlockSpec((None, rt * GRID_W, GW), lambda bi, i: (bi, i, 0)),
        out_shape=jax.ShapeDtypeStruct((b, n, GW), BF16),
        compiler_params=_params("parallel", "parallel"), name="na",
    )(q, k, v, kc, vc, bias)


def _na_bias_table(rel_bias, rows):
    k_r = min(NA_ROWS, rows)
    cols = np.arange(GRID_W)
    col_start = np.clip(cols - NA_COLS // 2, 0, GRID_W - NA_COLS)
    inside = (cols[None, :] >= col_start[:, None]) & (cols[None, :] < col_start[:, None] + NA_COLS)
    rel_c = cols[None, :] - cols[:, None] + (NA_COLS - 1)
    rel_r = np.arange(k_r)[None, :] - np.arange(k_r)[:, None] + (NA_ROWS - 1)
    pick_r = (rel_r[:, :, None] == np.arange(2 * NA_ROWS - 1)).astype(np.float32)
    pick_c = ((rel_c[:, :, None] == np.arange(2 * NA_COLS - 1)) & inside[:, :, None]).astype(np.float32)
    t = jnp.einsum("cai,lhim->lcham", pick_r, rel_bias.astype(F32), precision=HIGHEST)
    t = jnp.einsum("lcham,jkm->lchjak", t, pick_c, precision=HIGHEST)
    t = t + np.where(inside, 0.0, MASK_VALUE).astype(np.float32)[None, None, None, :, None, :]
    return (t * LOG2E).reshape(rel_bias.shape[0], k_r, NA_HEADS * GRID_W, k_r * GRID_W)


def _s5_scan(hs_ref, st_ref, a_ref, *, tc, nb, n_state, col_w, reverse):
    for c0 in range(0, n_state, col_w):
        re = slice(c0, c0 + col_w)
        im = slice(n_state + c0, n_state + c0 + col_w)
        ar = jnp.broadcast_to(a_ref[:, re], (nb, col_w))
        ai = jnp.broadcast_to(a_ref[:, im], (nb, col_w))

        hr, hi = st_ref[:, re], st_ref[:, im]
        for t in range(tc):
            rows = slice((tc - 1 - t if reverse else t) * nb, (tc - t if reverse else t + 1) * nb)
            hr, hi = ar * hr - ai * hi + hs_ref[rows, re], ar * hi + ai * hr + hs_ref[rows, im]
            hs_ref[rows, re] = hr
            hs_ref[rows, im] = hi
        st_ref[:, re] = hr
        st_ref[:, im] = hi


def _s5_fwd_body(uc_ref, ul_ref, perm_ref, bd_ref, cd_ref, a_ref, ut_ref, yf_ref, hs0_ref, hs1_ref, st_ref,
                 *, n_c, scan):
    j = pl.program_id(0)

    @pl.when(j == 0)
    def _():
        st_ref[...] = jnp.zeros_like(st_ref)
        hs1_ref[...] = jnp.zeros_like(hs1_ref)

    def step(cur, prev):
        nb, tc, w = uc_ref.shape
        u_bm = jnp.where(j < n_c, uc_ref[...], ul_ref[...]).reshape(nb * tc, w)
        ut = _dot2(perm_ref[...], u_bm).astype(ut_ref.dtype)
        ut_ref[...] = ut
        cur[...] = _dot2(ut, bd_ref[...])
        scan(prev, st_ref, a_ref, reverse=False)
        yf_ref[...] = _dot2(prev[...].astype(BF16), cd_ref[...])

    pl.when(j % 2 == 0)(lambda: step(hs0_ref, hs1_ref))
    pl.when(j % 2 == 1)(lambda: step(hs1_ref, hs0_ref))


def _s5_bwd_body(ut_ref, utp_ref, yf_ref, permt_ref, bd_ref, cd_ref, a_ref, dsk_ref, wglu_ref, oc_ref, ol_ref,
                 hs0_ref, hs1_ref, st_ref, *, n_c, scan):
    j = pl.program_id(0)

    @pl.when(j == 0)
    def _():
        st_ref[...] = jnp.zeros_like(st_ref)
        hs1_ref[...] = jnp.zeros_like(hs1_ref)

    def step(cur, prev):
        cur[...] = _dot2(ut_ref[...], bd_ref[...])
        scan(prev, st_ref, a_ref, reverse=True)
        t = (dsk_ref[...] * utp_ref[...].astype(F32) + yf_ref[...]
             + _dot2(prev[...].astype(BF16), cd_ref[...]))
        g = jax.nn.gelu(t)
        ob = g * jax.nn.sigmoid(_dot2(g.astype(BF16), wglu_ref[...]))
        ob = _dot2(permt_ref[...], ob.astype(BF16)).astype(BF16)

        @pl.when(j <= n_c)
        def _():
            oc_ref[...] = ob.reshape(oc_ref.shape)

        @pl.when(j > n_c)
        def _():
            ol_ref[...] = ob.reshape(ol_ref.shape)

    pl.when(j % 2 == 0)(lambda: step(hs0_ref, hs1_ref))
    pl.when(j % 2 == 1)(lambda: step(hs1_ref, hs0_ref))


def _s5(usc, us, bd, cd, a_row, dsk, wglu, layer, tc):
    nb, lc, _ = usc.shape
    n = us.shape[1]
    n_c, n_l = lc // tc, n // tc
    n_all = n_c + n_l
    n_state2 = bd.shape[-1]
    rows = tc * nb
    r = np.arange(rows)
    perm = np.zeros((rows, rows), np.float32)
    perm[r, (r % nb) * tc + r // nb] = 1.0
    scan = functools.partial(_s5_scan, tc=tc, nb=nb, n_state=n_state2 // 2, col_w=512)
    scratch = [pltpu.VMEM((rows, n_state2), F32), pltpu.VMEM((nb, n_state2), F32)]
    seg = lambda idx: pl.BlockSpec((nb, tc, GW), lambda j: (0, idx(j), 0))
    tm_rows = lambda idx: pl.BlockSpec((rows, GW), lambda j: (idx(j), 0))

    ut, yf = pl.pallas_call(
        functools.partial(_s5_fwd_body, n_c=n_c, scan=scan), grid=(n_all + 1,),
        in_specs=[seg(lambda j: jnp.minimum(j, n_c - 1)), seg(lambda j: jnp.clip(j - n_c, 0, n_l - 1)),
                  _const_spec(perm.shape)] + [_layer_spec(t, layer, 0) for t in (bd, cd, a_row)],
        out_specs=[tm_rows(lambda j: jnp.minimum(j, n_all - 1)), tm_rows(lambda j: jnp.maximum(j - 1, 0))],
        out_shape=[jax.ShapeDtypeStruct((n_all * rows, GW), BF16), jax.ShapeDtypeStruct((n_all * rows, GW), F32)],
        scratch_shapes=[scratch[0], scratch[0], scratch[1]],
        compiler_params=_params("arbitrary"), name="s5_fwd",
    )(usc, us, jnp.asarray(perm.astype(BF16)), bd, cd, a_row)

    def blk(j):
        j = jnp.clip(j, 0, n_all - 1)
        return jnp.where(j < n_c, n_c - 1 - j, n_c + n_all - 1 - j)

    prev = lambda j: jnp.maximum(j - 1, 0)
    return pl.pallas_call(
        functools.partial(_s5_bwd_body, n_c=n_c, scan=scan), grid=(n_all + 1,),
        in_specs=[tm_rows(blk), tm_rows(lambda j: blk(prev(j))), tm_rows(lambda j: blk(prev(j))),
                  _const_spec(perm.shape)] + [_layer_spec(t, layer, 1) for t in (bd, cd, a_row)]
        + [_layer_spec(dsk, layer), _layer_spec(wglu, layer)],
        out_specs=[seg(lambda j: jnp.maximum(n_c - 1 - prev(j), 0)),
                   seg(lambda j: n_l - 1 - jnp.maximum(prev(j) - n_c, 0))],
        out_shape=[jax.ShapeDtypeStruct(usc.shape, BF16), jax.ShapeDtypeStruct(us.shape, BF16)],
        scratch_shapes=[scratch[0], scratch[0], scratch[1]], compiler_params=_params("arbitrary"), name="s5_bwd",
    )(ut, ut, yf, jnp.asarray(perm.T.astype(BF16)), bd, cd, a_row, dsk, wglu)


def _s5_matrices(abr, abi, bbr, bbi, c_re, c_im):
    g, p, h = S5_GROUPS, S5_STATE, S5_GROUP_CH
    lead = abr.shape[:2]
    eye = np.eye(g, dtype=np.float32)[None, None, :, None, :, None]
    spread = lambda t: (jnp.swapaxes(t, -1, -2)[:, :, :, :, None, :] * eye).reshape(
        *lead, g * t.shape[-1], g * t.shape[-2])
    bd = jnp.concatenate([spread(bbr), spread(bbi)], axis=-1)
    cd = jnp.concatenate([spread(c_re.astype(F32)), spread(-c_im.astype(F32))], axis=-2)
    a_row = jnp.concatenate([abr.reshape(*lead, 1, g * p), abi.reshape(*lead, 1, g * p)], axis=-1)
    return bd.astype(BF16), cd.astype(BF16), a_row


def _fnet_body(u_ref, uf_ref, dft_ref, ccs_ref, w_ref, b_ref, o_ref, ue_ref):
    h = u_ref.shape[0]

    @pl.when(pl.program_id(1) == 0)
    def _():
        t = jnp.dot(u_ref[...], ccs_ref[...], preferred_element_type=F32)
        tf = jnp.dot(uf_ref[...], ccs_ref[...], preferred_element_type=F32)
        row0 = lax.broadcasted_iota(jnp.int32, (h, GW), 0) == 0
        ue_ref[0:h, :] = (t[:, :GW] + jnp.where(row0, 0.0, tf[:, :GW])).astype(BF16)
        ue_ref[h:2 * h, :] = jnp.where(row0, tf[:, :GW], t[:, GW:] - tf[:, GW:]).astype(BF16)

    tk = o_ref.shape[0]
    rows = pl.ds(pl.multiple_of(pl.program_id(1) * tk, tk), tk)
    f = _dot2(dft_ref[rows, :], ue_ref[...])
    o = jnp.dot(f.astype(BF16), w_ref[...], preferred_element_type=F32) + b_ref[...]
    o_ref[...] = o.astype(o_ref.dtype)


def _fnet(u, dft, ccs, w, bias, layer, tk):
    b, n, _ = u.shape
    h = n // 2
    uf = jnp.concatenate([u[:, h:h + 1], jnp.flip(u[:, h + 1:], axis=1)], axis=1)
    half = pl.BlockSpec((None, h, GW), lambda bi, k: (bi, 0, 0))
    return pl.pallas_call(
        _fnet_body, grid=(b, n // tk),
        in_specs=[half, half, _const_spec(dft.shape),
                  _const_spec(ccs.shape), _layer_spec(w, layer), _layer_spec(bias, layer)],
        out_specs=pl.BlockSpec((None, tk, GW), lambda bi, k: (bi, k, 0)),
        out_shape=jax.ShapeDtypeStruct((b, n, GW), BF16),
        scratch_shapes=[pltpu.VMEM((n, GW), BF16)],
        compiler_params=_params("parallel", "arbitrary"), name="fnet",
    )(u, uf, dft, ccs, w, bias)


def _dft_matrices(n):
    ch = GW // FN_GROUPS
    h = n // 2
    k = np.arange(n)
    ang = (2.0 * np.pi / n) * ((k[:, None] * np.arange(h)[None, :]) % n)
    m = -np.sin(ang)
    m[:, 0] = 1.0 - 2.0 * (k % 2)
    dft = np.concatenate([np.cos(ang), m], axis=1)
    c = np.arange(GW)
    same = (c[:, None] // ch) == (c[None, :] // ch)
    angc = (2.0 * np.pi / ch) * (((c[:, None] % ch) * (c[None, :] % ch)) % ch)
    norm = 1.0 / math.sqrt(n * ch)
    ccs = np.concatenate([np.where(same, np.cos(angc), 0.0), np.where(same, np.sin(angc), 0.0)], axis=1) * norm
    return (jnp.asarray(dft.astype(np.float32)).astype(BF16),
            jnp.asarray(ccs.astype(np.float32)).astype(BF16))


def _out_mlp_body(x_ref, oa_ref, ob_ref, on_ref, od_ref, g1_ref, sh2_ref, sc2_ref, g2_ref,
                  wout_ref, gpm_ref, gpre_ref, gpost_ref, w1_ref, w2_ref, o_ref, *, ff_chunk):
    cat = jnp.concatenate([oa_ref[...], ob_ref[...], on_ref[...], od_ref[...]], axis=-1)
    y = jnp.dot(cat, wout_ref[...], preferred_element_type=F32)
    x1 = x_ref[...] + g1_ref[...] * _rms(y, gpm_ref[...])
    h = (_rms(x1, gpre_ref[...]) * (1.0 + sc2_ref[...]) + sh2_ref[...]).astype(BF16)
    d_ff = w1_ref.shape[1]
    m = jnp.zeros(x1.shape, F32)
    for c0 in range(0, d_ff, ff_chunk):
        a = jnp.maximum(jnp.dot(h, w1_ref[:, c0:c0 + ff_chunk], preferred_element_type=F32), 0.0)
        m = m + jnp.dot((a * a).astype(BF16), w2_ref[c0:c0 + ff_chunk, :], preferred_element_type=F32)
    o_ref[...] = x1 + g2_ref[...] * _rms(m, gpost_ref[...])


def _out_mlp(x, oa, ob, on, od, mod_t, layer, mod_row, wout, gpm, gpre, gpost, w1, w2, tm):
    b, n, d = x.shape
    row = lambda w: pl.BlockSpec((None, tm, w), lambda bi, i: (bi, i, 0))
    consts = [wout, gpm, gpre, gpost, w1, w2]
    return pl.pallas_call(
        functools.partial(_out_mlp_body, ff_chunk=1024),
        grid=(b, n // tm),
        in_specs=[row(d), row(GW), row(GW), row(GW), row(GW)]
        + [_mod_spec(mod_t, layer, which, mod_row) for which in (2, 3, 4, 5)]
        + [_layer_spec(t, layer) for t in consts],
        out_specs=row(d), out_shape=jax.ShapeDtypeStruct((b, n, d), F32),
        compiler_params=_params("parallel", "parallel"), name="out_mlp",
    )(x, oa, ob, on, od, mod_t, mod_t, mod_t, mod_t, *consts)


def _rope_tables(n):
    half = HEAD_DIM // 4
    pos = np.arange(n)
    jj = np.arange(GW) % HEAD_DIM
    inv = ROPE_BASE ** (-(jj % half) / half)
    p = np.where((jj // (2 * half) == 0)[None, :], (pos // GRID_W)[:, None], (pos % GRID_W)[:, None])
    ang = p * inv[None, :]
    first = ((jj % (2 * half)) < half)[None, :]
    sin = np.sin(ang)
    tabs = (np.cos(ang), np.where(first, -sin, 0.0), np.where(first, 0.0, sin))
    return tuple(jnp.asarray(t.astype(np.float32)) for t in tabs)


def kernel(x, c, ctx, c_ctx, w_ada, b_ada, g_pre_mix, g_post_mix, g_pre_mlp, g_post_mlp, w_in, g_q_attn, g_k_attn,
           s5_a_re, s5_a_im, s5_log_dt, s5_b_re, s5_b_im, s5_c_re, s5_c_im, s5_d, w_s5_glu, na_rel_bias,
           w_fnet, b_fnet, w_out, w_mlp1, w_mlp2):
    b, n, d = x.shape
    lc = ctx.shape[1]
    depth = w_ada.shape[0]
    rows = n // GRID_W

    n_rows = -(-(b + 1) // 8) * 8
    cc = jnp.concatenate([c, c_ctx[None, :], jnp.zeros((n_rows - b - 1, d), c.dtype)], axis=0)
    mod_t = _ada(cc, w_ada, b_ada)

    vec = lambda t: t.reshape(depth, 1, -1).astype(F32)
    abr, abi, bbr, bbi = _s5_prep(s5_a_re, s5_a_im, s5_log_dt, s5_b_re, s5_b_im)
    bd, cd, a_row = _s5_matrices(abr, abi, bbr, bbi, s5_c_re, s5_c_im)
    na_bias = _na_bias_table(na_rel_bias, rows)
    w_in_b, w_glu_b, w_fn_b = w_in.astype(BF16), w_s5_glu.astype(BF16), w_fnet.astype(BF16)
    mlp_consts = (w_out.astype(BF16), vec(g_post_mix), vec(g_pre_mlp), vec(g_post_mlp),
                  w_mlp1.astype(BF16), w_mlp2.astype(BF16))
    g_pre, d_skip, b_fn = vec(g_pre_mix), vec(s5_d), vec(b_fnet)
    gq = jnp.tile(vec(g_q_attn), (1, 1, ATT_HEADS))
    gk = jnp.tile(vec(g_k_attn), (1, 1, ATT_KV_HEADS))
    rope_tabs = _rope_tables(n)
    lane = np.arange(GW)
    hm = jnp.asarray(np.where((lane[:, None] // HEAD_DIM) == (lane[None, :] // HEAD_DIM), 1.0 / HEAD_DIM, 0.0)
                     .astype(np.float32).astype(BF16))
    dft_n, ccs_n = _dft_matrices(n)
    dft_c, ccs_c = _dft_matrices(lc)

    tm_in = min(512, n)
    tm_c = min(256, lc)
    tc = math.gcd(64, math.gcd(n, lc))
    group = ATT_HEADS // ATT_KV_HEADS

    xc = ctx
    for l in range(depth):
        need_ctx = l < depth - 1
        qa, ka, va, us, nq, nk, nv, fn = _in_proj(x, mod_t, l, None, g_pre, w_in_b, gq, gk, hm, rope_tabs, tm_in)
        qac, kac, vac, usc, nqc, nkc, nvc, fnc = _in_proj(xc, mod_t, l, b, g_pre, w_in_b, gq, gk, hm, None, tm_c)

        oa = _gqa(qa, ka, va, kac, vac, group=group, tq=min(256, n))
        on = _na(nq, nk, nv, nkc, nvc, na_bias, l, rt=min(8, rows))
        od = _fnet(fn, dft_n, ccs_n, w_fn_b, b_fn, l, tk=min(512, n))
        obc, ob = _s5(usc, us, bd, cd, a_row, d_skip, w_glu_b, l, tc)
        x_new = _out_mlp(x, oa, ob, on, od, mod_t, l, None, *mlp_consts, tm_in)
        if need_ctx:
            oac = _gqa(qac, kac, vac, group=group, tq=tm_c)
            onc = _mha(nqc, nkc, nvc, tq=tm_c)
            odc = _fnet(fnc, dft_c, ccs_c, w_fn_b, b_fn, l, tk=tm_c)
            xc = _out_mlp(xc, oac, obc, onc, odc, mod_t, l, b, *mlp_consts, tm_c)
        x = x_new
    return x
```

```python
import functools
import math

import jax
import jax.numpy as jnp
import numpy as np
from jax import lax
from jax.experimental import pallas as pl
from jax.experimental.pallas import tpu as pltpu

F32 = jnp.float32
BF16 = jnp.bfloat16
HIGHEST = lax.Precision.HIGHEST

HEAD_DIM = 64
GRID_W = 64
N_MOD = 6
EPS = 1e-6
ATT_HEADS = 4
ATT_KV_HEADS = 2
ROPE_BASE = 10000.0
S5_GROUPS = 16
S5_GROUP_CH = 16
S5_STATE = 64
S5_MIN_DECAY = 1e-4
NA_HEADS = 4
NA_ROWS = 8
NA_COLS = 16
FN_GROUPS = 4
GW = 256
MASK_VALUE = -1e30
LOG2E = 1.4426950408889634
KEY_CHUNK = 1024

OFF_ATT_Q, OFF_ATT_K, OFF_ATT_V, OFF_S5 = 0, 256, 384, 512
OFF_NA_Q, OFF_NA_K, OFF_NA_V, OFF_FN, IN_WIDTH = 768, 1024, 1280, 1536, 1792

VMEM_LIMIT = 56 * 1024 * 1024


def _params(*sem):
    return pltpu.CompilerParams(dimension_semantics=sem, vmem_limit_bytes=VMEM_LIMIT)


def _const_spec(shape):
    return pl.BlockSpec(shape, lambda *_: (0,) * len(shape), pipeline_mode=pl.Buffered(1))


def _layer_spec(arr, *lead):
    k = len(lead)
    return pl.BlockSpec((None,) * k + arr.shape[k:], lambda *_: tuple(lead) + (0,) * (arr.ndim - k),
                        pipeline_mode=pl.Buffered(1))


def _rms(x, g):
    return x * lax.rsqrt(jnp.mean(x * x, axis=-1, keepdims=True) + EPS) * g


def _dot2(a, b):
    half = a.shape[0] // 2
    return jnp.concatenate([jnp.dot(a[:half], b, preferred_element_type=F32),
                            jnp.dot(a[half:], b, preferred_element_type=F32)], axis=0)


def _nt_dot(a, b):
    return lax.dot_general(a, b, (((1,), (1,)), ((), ())), preferred_element_type=F32)


def _ada_body(c_ref, w_ref, b_ref, o_ref):
    c = c_ref[...]
    s = c * jax.nn.sigmoid(c)
    o_ref[...] = jnp.dot(s, w_ref[...], precision=HIGHEST, preferred_element_type=F32) + b_ref[...]


def _ada(cc, w_ada, b_ada):
    depth, d, n_out = w_ada.shape
    rows = cc.shape[0]
    out = pl.pallas_call(
        _ada_body,
        grid=(depth, n_out // d),
        in_specs=[
            pl.BlockSpec((rows, d), lambda l, j: (0, 0)),
            pl.BlockSpec((None, d, d), lambda l, j: (l, 0, j)),
            pl.BlockSpec((None, 1, d), lambda l, j: (l, 0, j)),
        ],
        out_specs=pl.BlockSpec((None, None, rows, d), lambda l, j: (l, j, 0, 0)),
        out_shape=jax.ShapeDtypeStruct((depth, n_out // d, rows, d), F32),
        compiler_params=_params("parallel", "parallel"),
        name="ada",
    )(cc, w_ada, b_ada.reshape(depth, 1, n_out))
    return out.reshape(depth, n_out // d, rows, 1, d)


def _mod_spec(mod_t, layer, which, row):
    d = mod_t.shape[-1]
    return pl.BlockSpec((None, None, None, 1, d),
                        lambda bi, i: (layer, which, bi if row is None else row, 0, 0))


def _s5_prep_body(are_ref, aim_ref, ldt_ref, are_b_ref, aim_b_ref, ldt_b_ref, bre_ref, bim_ref,
                  abr_ref, abi_ref, bbr_ref, bbi_ref):
    def zoh(a_re, a_im, log_dt):
        lr = jnp.minimum(a_re, -S5_MIN_DECAY)
        dt = jnp.exp(log_dt)
        mag = jnp.exp(lr * dt)
        ab_r = mag * jnp.cos(a_im * dt)
        ab_i = mag * jnp.sin(a_im * dt)
        return lr, ab_r, ab_i

    _, ab_r, ab_i = zoh(are_ref[...], aim_ref[...], ldt_ref[...])
    abr_ref[...] = ab_r
    abi_ref[...] = ab_i
    lr, ab_r, ab_i = zoh(are_b_ref[...], aim_b_ref[...], ldt_b_ref[...])
    li = aim_b_ref[...]
    nr, ni = ab_r - 1.0, ab_i
    den = lr * lr + li * li
    kr = (nr * lr + ni * li) / den
    ki = (ni * lr - nr * li) / den
    br, bi = bre_ref[...], bim_ref[...]
    bbr_ref[...] = kr * br - ki * bi
    bbi_ref[...] = kr * bi + ki * br


def _s5_prep(a_re, a_im, log_dt, b_re, b_im):
    lead = a_re.shape[:3]
    r = lead[0] * lead[1] * lead[2]
    p, h = S5_STATE, S5_GROUP_CH
    a2 = lambda t: t.reshape(r, p)
    ab = lambda t: jnp.broadcast_to(t.reshape(r, p, 1), (r, p, h)).reshape(r, p * h)
    ldt = jnp.broadcast_to(log_dt.reshape(r, 1), (r, p))
    ldt_b = jnp.broadcast_to(log_dt.reshape(r, 1), (r, p * h))
    small = jax.ShapeDtypeStruct((r, p), F32)
    big = jax.ShapeDtypeStruct((r, p * h), F32)
    abr, abi, bbr, bbi = pl.pallas_call(
        _s5_prep_body, out_shape=(small, small, big, big), name="s5_prep",
    )(a2(a_re), a2(a_im), ldt, ab(a_re), ab(a_im), ldt_b, b_re.reshape(r, p * h), b_im.reshape(r, p * h))
    return (abr.reshape(*lead, p), abi.reshape(*lead, p),
            bbr.reshape(*lead, p, h), bbi.reshape(*lead, p, h))


def _in_proj_body(*refs, rope, sub):
    if rope:
        (x_ref, sh_ref, sc_ref, g_ref, w_ref, gq_ref, gk_ref, hm_ref, cos_ref, sa_ref, sb_ref,
         qa_ref, ka_ref, va_ref, us_ref, nq_ref, nk_ref, nv_ref, fn_ref) = refs
    else:
        (x_ref, sh_ref, sc_ref, g_ref, w_ref, gq_ref, gk_ref, hm_ref,
         qa_ref, ka_ref, va_ref, us_ref, nq_ref, nk_ref, nv_ref, fn_ref) = refs
    hm = hm_ref[...]
    kw = OFF_ATT_V - OFF_ATT_K
    scale = HEAD_DIM ** -0.5 * LOG2E
    lane = lax.broadcasted_iota(jnp.int32, (sub, HEAD_DIM), 1)
    one_col = jnp.where(lane == 0, 1.0, 0.0)

    def head_norm(t, g, avg):
        ms = jnp.dot((t * t).astype(BF16), avg, preferred_element_type=F32)
        return t * lax.rsqrt(ms + EPS) * g

    for r0 in range(0, x_ref.shape[0], sub):
        rs = slice(r0, r0 + sub)
        h = _rms(x_ref[rs, :], g_ref[...]) * (1.0 + sc_ref[...]) + sh_ref[...]
        hb = h.astype(BF16)
        z_lo = jnp.dot(hb, w_ref[:, :OFF_NA_K], preferred_element_type=F32)
        z_hi = jnp.dot(hb, w_ref[:, OFF_NA_K:], preferred_element_type=F32)
        q = head_norm(z_lo[:, OFF_ATT_Q:OFF_ATT_K], gq_ref[...], hm)
        k = head_norm(z_lo[:, OFF_ATT_K:OFF_ATT_V], gk_ref[...], hm[:kw, :kw])
        if rope:
            def rot(t):
                w = t.shape[-1]
                half = HEAD_DIM // 4
                return (t * cos_ref[rs, :w] + pltpu.roll(t, w - half, 1) * sa_ref[rs, :w]
                        + pltpu.roll(t, half, 1) * sb_ref[rs, :w])
            q, k = rot(q), rot(k)
        qa_ref[rs, :] = (q * scale).astype(qa_ref.dtype)
        ka_ref[rs, :] = k.astype(ka_ref.dtype)
        v = z_lo[:, OFF_ATT_V:OFF_S5]
        v_ext = jnp.concatenate(
            sum(([v[:, HEAD_DIM * i:HEAD_DIM * (i + 1)], one_col] for i in range(ATT_KV_HEADS)), []), axis=-1)
        va_ref[:, rs] = v_ext.T.astype(va_ref.dtype)
        us_ref[rs, :] = z_lo[:, OFF_S5:OFF_NA_Q].astype(us_ref.dtype)
        nq_ref[rs, :] = (z_lo[:, OFF_NA_Q:OFF_NA_K] * scale).astype(nq_ref.dtype)
        nk_ref[rs, :] = z_hi[:, :GW].astype(nk_ref.dtype)
        nv_ref[rs, :] = z_hi[:, GW:2 * GW].astype(nv_ref.dtype)
        fn_ref[rs, :] = z_hi[:, 2 * GW:].astype(fn_ref.dtype)


def _in_proj(x, mod_t, layer, mod_row, g_pre, w_in, gq, gk, hm, rope_tabs, tm):
    b, n, d = x.shape
    rope = rope_tabs is not None
    row = lambda w: pl.BlockSpec((None, tm, w), lambda bi, i: (bi, i, 0))
    in_specs = [row(d), _mod_spec(mod_t, layer, 0, mod_row), _mod_spec(mod_t, layer, 1, mod_row)]
    in_specs += [_layer_spec(t, layer) for t in (g_pre, w_in, gq, gk)] + [_const_spec(hm.shape)]
    args = [x, mod_t, mod_t, g_pre, w_in, gq, gk, hm]
    if rope:
        in_specs += [pl.BlockSpec((tm, GW), lambda bi, i: (i, 0))] * 3
        args += list(rope_tabs)
    kvw = ATT_KV_HEADS * HEAD_DIM
    widths = [GW, kvw, None, GW, GW, GW, GW, GW]
    out_specs = [pl.BlockSpec((None, 2 * kvw, tm), lambda bi, i: (bi, 0, i)) if w is None else row(w) for w in widths]
    out_shape = [jax.ShapeDtypeStruct((b, 2 * kvw, n) if w is None else (b, n, w), BF16) for w in widths]
    return pl.pallas_call(
        functools.partial(_in_proj_body, rope=rope, sub=min(256, tm)),
        grid=(b, n // tm), in_specs=in_specs, out_specs=out_specs, out_shape=out_shape,
        compiler_params=_params("parallel", "parallel"), name="in_proj_rope" if rope else "in_proj",
    )(*args)


def _col_max(s):
    for rows in (256, 64):
        if s.shape[0] > rows and s.shape[0] % rows == 0:
            s = jnp.max(s.reshape(s.shape[0] // rows, rows, s.shape[1]), axis=0)
    return jnp.max(s, axis=0, keepdims=True)


def _gqa_body(*refs, n_heads, group, two_sets):
    if two_sets:
        q_ref, k1_ref, v1t_ref, k2_ref, v2t_ref, o_ref = refs
    else:
        q_ref, k1_ref, v1t_ref, o_ref = refs
    key_sets = [(k1_ref, v1t_ref)] + ([(k2_ref, v2t_ref)] if two_sets else [])

    def scores(h):
        ks = slice(HEAD_DIM * (h // group), HEAD_DIM * (h // group + 1))
        qh = q_ref[:, HEAD_DIM * h:HEAD_DIM * (h + 1)]
        s = [_nt_dot(k_ref[:, ks], qh).astype(BF16) for k_ref, _ in key_sets]
        m = functools.reduce(jnp.maximum, [_col_max(t) for t in s])
        return s, m

    def weighted_values(h, s, m):
        vr = slice(2 * HEAD_DIM * (h // group), 2 * HEAD_DIM * (h // group + 1))
        o = None
        for t, (_, vt_ref) in zip(s, key_sets):
            for c0 in range(0, t.shape[0], KEY_CHUNK):
                p = jnp.exp2(t[c0:c0 + KEY_CHUNK] - m)
                part = jnp.dot(vt_ref[vr, c0:c0 + KEY_CHUNK], p, preferred_element_type=F32)
                o = part if o is None else o + part
        return o[:HEAD_DIM] / o[HEAD_DIM:HEAD_DIM + 1]

    outs = []
    nxt = scores(0)
    for h in range(n_heads):
        cur, nxt = nxt, (scores(h + 1) if h + 1 < n_heads else None)
        outs.append(weighted_values(h, *cur))
    o_ref[...] = jnp.concatenate(outs, axis=0).T.astype(o_ref.dtype)


def _gqa(q, k1, v1t, k2=None, v2t=None, *, group, tq):
    b, nq, qw = q.shape
    two_sets = k2 is not None
    full = lambda t: pl.BlockSpec((None,) + t.shape[1:], lambda bi, i: (bi, 0, 0))
    args = [q, k1, v1t] + ([k2, v2t] if two_sets else [])
    in_specs = [pl.BlockSpec((None, tq, qw), lambda bi, i: (bi, i, 0))] + [full(t) for t in args[1:]]
    return pl.pallas_call(
        functools.partial(_gqa_body, n_heads=qw // HEAD_DIM, group=group, two_sets=two_sets),
        grid=(b, nq // tq), in_specs=in_specs,
        out_specs=pl.BlockSpec((None, tq, qw), lambda bi, i: (bi, i, 0)),
        out_shape=jax.ShapeDtypeStruct((b, nq, qw), BF16),
        compiler_params=_params("parallel", "parallel"), name="gqa2" if two_sets else "gqa1",
    )(*args)


def _mha_body(q_ref, k_ref, v_ref, o_ref, *, n_heads):
    for h in range(n_heads):
        hs = slice(HEAD_DIM * h, HEAD_DIM * (h + 1))
        s = _nt_dot(q_ref[:, hs], k_ref[:, hs])
        p = jnp.exp2(s - jnp.max(s, axis=-1, keepdims=True))
        o = jnp.dot(p.astype(BF16), v_ref[:, hs], preferred_element_type=F32)
        o_ref[:, hs] = (o / jnp.sum(p, axis=-1, keepdims=True)).astype(o_ref.dtype)


def _mha(q, k, v, *, tq):
    b, nq, qw = q.shape
    full = lambda t: pl.BlockSpec((None,) + t.shape[1:], lambda bi, i: (bi, 0, 0))
    return pl.pallas_call(
        functools.partial(_mha_body, n_heads=qw // HEAD_DIM),
        grid=(b, nq // tq),
        in_specs=[pl.BlockSpec((None, tq, qw), lambda bi, i: (bi, i, 0)), full(k), full(v)],
        out_specs=pl.BlockSpec((None, tq, qw), lambda bi, i: (bi, i, 0)),
        out_shape=jax.ShapeDtypeStruct((b, nq, qw), BF16),
        compiler_params=_params("parallel", "parallel"), name="mha",
    )(q, k, v)


def _na_body(q_ref, k_ref, v_ref, kc_ref, vc_ref, bias_ref, o_ref, *, rt, k_r, rows):
    i = pl.program_id(1)
    kc = kc_ref[...]
    vc = vc_ref[...]
    lane_head = lax.broadcasted_iota(jnp.int32, (GRID_W, GW), 1) // HEAD_DIM
    def scores(j):
        r = i * rt + j
        rs = jnp.clip(r - k_r // 2, 0, rows - k_r)
        start = pl.multiple_of(rs * GRID_W, GRID_W)
        q = q_ref[j * GRID_W:(j + 1) * GRID_W, :]
        q4 = jnp.concatenate([jnp.where(lane_head == h, q, jnp.zeros_like(q)) for h in range(NA_HEADS)], axis=0)
        s_loc = (_nt_dot(q4, k_ref[pl.ds(start, k_r * GRID_W), :]) + bias_ref[r - rs]).astype(BF16)
        s_ctx = _nt_dot(q4, kc).astype(BF16)
        return start, s_loc, s_ctx

    def weighted_values(j, start, s_loc, s_ctx):
        m = jnp.maximum(jnp.max(s_loc, axis=-1, keepdims=True), jnp.max(s_ctx, axis=-1, keepdims=True))
        p_loc = jnp.exp2(s_loc - m)
        p_ctx = jnp.exp2(s_ctx - m)
        l = (jnp.sum(p_loc.astype(F32), axis=-1, keepdims=True)
             + jnp.sum(p_ctx.astype(F32), axis=-1, keepdims=True))
        o4 = (jnp.dot(p_loc, v_ref[pl.ds(start, k_r * GRID_W), :], preferred_element_type=F32)
              + jnp.dot(p_ctx, vc, preferred_element_type=F32)) / l
        o = jnp.zeros((GRID_W, GW), F32)
        for h in range(NA_HEADS):
            o = o + jnp.where(lane_head == h, o4[h * GRID_W:(h + 1) * GRID_W, :], 0.0)
        o_ref[j * GRID_W:(j + 1) * GRID_W, :] = o.astype(o_ref.dtype)

    nxt = scores(0)
    for j in range(rt):
        cur, nxt = nxt, (scores(j + 1) if j + 1 < rt else None)
        weighted_values(j, *cur)


def _na(q, k, v, kc, vc, bias, layer, rt):
    b, n, _ = q.shape
    rows = n // GRID_W
    k_r = bias.shape[1]
    full = lambda t: pl.BlockSpec((None,) + t.shape[1:], lambda bi, i: (bi, 0, 0))
    return pl.pallas_call(
        functools.partial(_na_body, rt=rt, k_r=k_r, rows=rows),
        grid=(b, rows // rt),
        in_specs=[pl.BlockSpec((None, rt * GRID_W, GW), lambda bi, i: (bi, i, 0)),
                  full(k), full(v), full(kc), full(vc), _layer_spec(bias, layer)],
        out_specs=pl.BlockSpec((None, rt * GRID_W, GW), lambda bi, i: (bi, i, 0)),
        out_shape=jax.ShapeDtypeStruct((b, n, GW), BF16),
        compiler_params=_params("parallel", "parallel"), name="na",
    )(q, k, v, kc, vc, bias)


def _na_bias_table(rel_bias, rows):
    k_r = min(NA_ROWS, rows)
    cols = np.arange(GRID_W)
    col_start = np.clip(cols - NA_COLS // 2, 0, GRID_W - NA_COLS)
    inside = (cols[None, :] >= col_start[:, None]) & (cols[None, :] < col_start[:, None] + NA_COLS)
    rel_c = cols[None, :] - cols[:, None] + (NA_COLS - 1)
    rel_r = np.arange(k_r)[None, :] - np.arange(k_r)[:, None] + (NA_ROWS - 1)
    pick_r = (rel_r[:, :, None] == np.arange(2 * NA_ROWS - 1)).astype(np.float32)
    pick_c = ((rel_c[:, :, None] == np.arange(2 * NA_COLS - 1)) & inside[:, :, None]).astype(np.float32)
    t = jnp.einsum("cai,lhim->lcham", pick_r, rel_bias.astype(F32), precision=HIGHEST)
    t = jnp.einsum("lcham,jkm->lchjak", t, pick_c, precision=HIGHEST)
    t = t + np.where(inside, 0.0, MASK_VALUE).astype(np.float32)[None, None, None, :, None, :]
    return (t * LOG2E).reshape(rel_bias.shape[0], k_r, NA_HEADS * GRID_W, k_r * GRID_W)


def _s5_scan(hs_ref, st_ref, a_ref, *, tc, nb, n_state, col_w, reverse):
    for c0 in range(0, n_state, col_w):
        re = slice(c0, c0 + col_w)
        im = slice(n_state + c0, n_state + c0 + col_w)
        ar = jnp.broadcast_to(a_ref[:, re], (nb, col_w))
        ai = jnp.broadcast_to(a_ref[:, im], (nb, col_w))

        hr, hi = st_ref[:, re], st_ref[:, im]
        for t in range(tc):
            rows = slice((tc - 1 - t if reverse else t) * nb, (tc - t if reverse else t + 1) * nb)
            hr, hi = ar * hr - ai * hi + hs_ref[rows, re], ar * hi + ai * hr + hs_ref[rows, im]
            hs_ref[rows, re] = hr
            hs_ref[rows, im] = hi
        st_ref[:, re] = hr
        st_ref[:, im] = hi


def _s5_fwd_body(uc_ref, ul_ref, perm_ref, bd_ref, cd_ref, a_ref, ut_ref, yf_ref, hs0_ref, hs1_ref, st_ref,
                 *, n_c, scan):
    j = pl.program_id(0)

    @pl.when(j == 0)
    def _():
        st_ref[...] = jnp.zeros_like(st_ref)
        hs1_ref[...] = jnp.zeros_like(hs1_ref)

    def step(cur, prev):
        nb, tc, w = uc_ref.shape
        u_bm = jnp.where(j < n_c, uc_ref[...], ul_ref[...]).reshape(nb * tc, w)
        ut = _dot2(perm_ref[...], u_bm).astype(ut_ref.dtype)
        ut_ref[...] = ut
        cur[...] = _dot2(ut, bd_ref[...])
        scan(prev, st_ref, a_ref, reverse=False)
        yf_ref[...] = _dot2(prev[...].astype(BF16), cd_ref[...])

    pl.when(j % 2 == 0)(lambda: step(hs0_ref, hs1_ref))
    pl.when(j % 2 == 1)(lambda: step(hs1_ref, hs0_ref))


def _s5_bwd_body(ut_ref, utp_ref, yf_ref, permt_ref, bd_ref, cd_ref, a_ref, dsk_ref, wglu_ref, oc_ref, ol_ref,
                 hs0_ref, hs1_ref, st_ref, *, n_c, scan):
    j = pl.program_id(0)

    @pl.when(j == 0)
    def _():
        st_ref[...] = jnp.zeros_like(st_ref)
        hs1_ref[...] = jnp.zeros_like(hs1_ref)

    def step(cur, prev):
        cur[...] = _dot2(ut_ref[...], bd_ref[...])
        scan(prev, st_ref, a_ref, reverse=True)
        t = (dsk_ref[...] * utp_ref[...].astype(F32) + yf_ref[...]
             + _dot2(prev[...].astype(BF16), cd_ref[...]))
        g = jax.nn.gelu(t)
        ob = g * jax.nn.sigmoid(_dot2(g.astype(BF16), wglu_ref[...]))
        ob = _dot2(permt_ref[...], ob.astype(BF16)).astype(BF16)

        @pl.when(j <= n_c)
        def _():
            oc_ref[...] = ob.reshape(oc_ref.shape)

        @pl.when(j > n_c)
        def _():
            ol_ref[...] = ob.reshape(ol_ref.shape)

    pl.when(j % 2 == 0)(lambda: step(hs0_ref, hs1_ref))
    pl.when(j % 2 == 1)(lambda: step(hs1_ref, hs0_ref))


def _s5(usc, us, bd, cd, a_row, dsk, wglu, layer, tc):
    nb, lc, _ = usc.shape
    n = us.shape[1]
    n_c, n_l = lc // tc, n // tc
    n_all = n_c + n_l
    n_state2 = bd.shape[-1]
    rows = tc * nb
    r = np.arange(rows)
    perm = np.zeros((rows, rows), np.float32)
    perm[r, (r % nb) * tc + r // nb] = 1.0
    scan = functools.partial(_s5_scan, tc=tc, nb=nb, n_state=n_state2 // 2, col_w=512)
    scratch = [pltpu.VMEM((rows, n_state2), F32), pltpu.VMEM((nb, n_state2), F32)]
    seg = lambda idx: pl.BlockSpec((nb, tc, GW), lambda j: (0, idx(j), 0))
    tm_rows = lambda idx: pl.BlockSpec((rows, GW), lambda j: (idx(j), 0))

    ut, yf = pl.pallas_call(
        functools.partial(_s5_fwd_body, n_c=n_c, scan=scan), grid=(n_all + 1,),
        in_specs=[seg(lambda j: jnp.minimum(j, n_c - 1)), seg(lambda j: jnp.clip(j - n_c, 0, n_l - 1)),
                  _const_spec(perm.shape)] + [_layer_spec(t, layer, 0) for t in (bd, cd, a_row)],
        out_specs=[tm_rows(lambda j: jnp.minimum(j, n_all - 1)), tm_rows(lambda j: jnp.maximum(j - 1, 0))],
        out_shape=[jax.ShapeDtypeStruct((n_all * rows, GW), BF16), jax.ShapeDtypeStruct((n_all * rows, GW), F32)],
        scratch_shapes=[scratch[0], scratch[0], scratch[1]],
        compiler_params=_params("arbitrary"), name="s5_fwd",
    )(usc, us, jnp.asarray(perm.astype(BF16)), bd, cd, a_row)

    def blk(j):
        j = jnp.clip(j, 0, n_all - 1)
        return jnp.where(j < n_c, n_c - 1 - j, n_c + n_all - 1 - j)

    prev = lambda j: jnp.maximum(j - 1, 0)
    return pl.pallas_call(
        functools.partial(_s5_bwd_body, n_c=n_c, scan=scan), grid=(n_all + 1,),
        in_specs=[tm_rows(blk), tm_rows(lambda j: blk(prev(j))), tm_rows(lambda j: blk(prev(j))),
                  _const_spec(perm.shape)] + [_layer_spec(t, layer, 1) for t in (bd, cd, a_row)]
        + [_layer_spec(dsk, layer), _layer_spec(wglu, layer)],
        out_specs=[seg(lambda j: jnp.maximum(n_c - 1 - prev(j), 0)),
                   seg(lambda j: n_l - 1 - jnp.maximum(prev(j) - n_c, 0))],
        out_shape=[jax.ShapeDtypeStruct(usc.shape, BF16), jax.ShapeDtypeStruct(us.shape, BF16)],
        scratch_shapes=[scratch[0], scratch[0], scratch[1]], compiler_params=_params("arbitrary"), name="s5_bwd",
    )(ut, ut, yf, jnp.asarray(perm.T.astype(BF16)), bd, cd, a_row, dsk, wglu)


def _s5_matrices(abr, abi, bbr, bbi, c_re, c_im):
    g, p, h = S5_GROUPS, S5_STATE, S5_GROUP_CH
    lead = abr.shape[:2]
    eye = np.eye(g, dtype=np.float32)[None, None, :, None, :, None]
    spread = lambda t: (jnp.swapaxes(t, -1, -2)[:, :, :, :, None, :] * eye).reshape(
        *lead, g * t.shape[-1], g * t.shape[-2])
    bd = jnp.concatenate([spread(bbr), spread(bbi)], axis=-1)
    cd = jnp.concatenate([spread(c_re.astype(F32)), spread(-c_im.astype(F32))], axis=-2)
    a_row = jnp.concatenate([abr.reshape(*lead, 1, g * p), abi.reshape(*lead, 1, g * p)], axis=-1)
    return bd.astype(BF16), cd.astype(BF16), a_row


def _fnet_body(u_ref, mirror_ref, dft_ref, ccs_ref, w_ref, b_ref, o_ref, ue_ref):
    h = u_ref.shape[0] // 2

    @pl.when(pl.program_id(1) == 0)
    def _():
        uf = jnp.dot(mirror_ref[...], u_ref[h:2 * h, :], preferred_element_type=F32).astype(BF16)
        t = jnp.dot(u_ref[0:h, :], ccs_ref[...], preferred_element_type=F32)
        tf = jnp.dot(uf, ccs_ref[...], preferred_element_type=F32)
        row0 = lax.broadcasted_iota(jnp.int32, (h, GW), 0) == 0
        ue_ref[0:h, :] = (t[:, :GW] + jnp.where(row0, 0.0, tf[:, :GW])).astype(BF16)
        ue_ref[h:2 * h, :] = jnp.where(row0, tf[:, :GW], t[:, GW:] - tf[:, GW:]).astype(BF16)

    tk = o_ref.shape[0]
    rows = pl.ds(pl.multiple_of(pl.program_id(1) * tk, tk), tk)
    f = _dot2(dft_ref[rows, :], ue_ref[...])
    o = jnp.dot(f.astype(BF16), w_ref[...], preferred_element_type=F32) + b_ref[...]
    o_ref[...] = o.astype(o_ref.dtype)


def _fnet(u, dft, ccs, w, bias, layer, tk):
    b, n, _ = u.shape
    h = n // 2
    mirror = np.zeros((h, h), np.float32)
    mirror[np.arange(h), (h - np.arange(h)) % h] = 1.0
    return pl.pallas_call(
        _fnet_body, grid=(b, n // tk),
        in_specs=[pl.BlockSpec((None, n, GW), lambda bi, k: (bi, 0, 0)), _const_spec(mirror.shape),
                  _const_spec(dft.shape),
                  _const_spec(ccs.shape), _layer_spec(w, layer), _layer_spec(bias, layer)],
        out_specs=pl.BlockSpec((None, tk, GW), lambda bi, k: (bi, k, 0)),
        out_shape=jax.ShapeDtypeStruct((b, n, GW), BF16),
        scratch_shapes=[pltpu.VMEM((n, GW), BF16)],
        compiler_params=_params("parallel", "arbitrary"), name="fnet",
    )(u, jnp.asarray(mirror.astype(BF16)), dft, ccs, w, bias)


def _dft_matrices(n):
    ch = GW // FN_GROUPS
    h = n // 2
    k = np.arange(n)
    ang = (2.0 * np.pi / n) * ((k[:, None] * np.arange(h)[None, :]) % n)
    m = -np.sin(ang)
    m[:, 0] = 1.0 - 2.0 * (k % 2)
    dft = np.concatenate([np.cos(ang), m], axis=1)
    c = np.arange(GW)
    same = (c[:, None] // ch) == (c[None, :] // ch)
    angc = (2.0 * np.pi / ch) * (((c[:, None] % ch) * (c[None, :] % ch)) % ch)
    norm = 1.0 / math.sqrt(n * ch)
    ccs = np.concatenate([np.where(same, np.cos(angc), 0.0), np.where(same, np.sin(angc), 0.0)], axis=1) * norm
    return (jnp.asarray(dft.astype(np.float32)).astype(BF16),
            jnp.asarray(ccs.astype(np.float32)).astype(BF16))


def _out_mlp_body(x_ref, oa_ref, ob_ref, on_ref, od_ref, g1_ref, sh2_ref, sc2_ref, g2_ref,
                  wout_ref, gpm_ref, gpre_ref, gpost_ref, w1_ref, w2_ref, o_ref, *, ff_chunk):
    cat = jnp.concatenate([oa_ref[...], ob_ref[...], on_ref[...], od_ref[...]], axis=-1)
    y = jnp.dot(cat, wout_ref[...], preferred_element_type=F32)
    x1 = x_ref[...] + g1_ref[...] * _rms(y, gpm_ref[...])
    h = (_rms(x1, gpre_ref[...]) * (1.0 + sc2_ref[...]) + sh2_ref[...]).astype(BF16)
    d_ff = w1_ref.shape[1]
    m = jnp.zeros(x1.shape, F32)
    for c0 in range(0, d_ff, ff_chunk):
        a = jnp.maximum(jnp.dot(h, w1_ref[:, c0:c0 + ff_chunk], preferred_element_type=F32), 0.0)
        m = m + jnp.dot((a * a).astype(BF16), w2_ref[c0:c0 + ff_chunk, :], preferred_element_type=F32)
    o_ref[...] = x1 + g2_ref[...] * _rms(m, gpost_ref[...])


def _out_mlp(x, oa, ob, on, od, mod_t, layer, mod_row, wout, gpm, gpre, gpost, w1, w2, tm):
    b, n, d = x.shape
    row = lambda w: pl.BlockSpec((None, tm, w), lambda bi, i: (bi, i, 0))
    consts = [wout, gpm, gpre, gpost, w1, w2]
    return pl.pallas_call(
        functools.partial(_out_mlp_body, ff_chunk=1024),
        grid=(b, n // tm),
        in_specs=[row(d), row(GW), row(GW), row(GW), row(GW)]
        + [_mod_spec(mod_t, layer, which, mod_row) for which in (2, 3, 4, 5)]
        + [_layer_spec(t, layer) for t in consts],
        out_specs=row(d), out_shape=jax.ShapeDtypeStruct((b, n, d), F32),
        compiler_params=_params("parallel", "parallel"), name="out_mlp",
    )(x, oa, ob, on, od, mod_t, mod_t, mod_t, mod_t, *consts)


def _rope_tables(n):
    half = HEAD_DIM // 4
    pos = np.arange(n)
    jj = np.arange(GW) % HEAD_DIM
    inv = ROPE_BASE ** (-(jj % half) / half)
    p = np.where((jj // (2 * half) == 0)[None, :], (pos // GRID_W)[:, None], (pos % GRID_W)[:, None])
    ang = p * inv[None, :]
    first = ((jj % (2 * half)) < half)[None, :]
    sin = np.sin(ang)
    tabs = (np.cos(ang), np.where(first, -sin, 0.0), np.where(first, 0.0, sin))
    return tuple(jnp.asarray(t.astype(np.float32)) for t in tabs)


def kernel(x, c, ctx, c_ctx, w_ada, b_ada, g_pre_mix, g_post_mix, g_pre_mlp, g_post_mlp, w_in, g_q_attn, g_k_attn,
           s5_a_re, s5_a_im, s5_log_dt, s5_b_re, s5_b_im, s5_c_re, s5_c_im, s5_d, w_s5_glu, na_rel_bias,
           w_fnet, b_fnet, w_out, w_mlp1, w_mlp2):
    b, n, d = x.shape
    lc = ctx.shape[1]
    depth = w_ada.shape[0]
    rows = n // GRID_W

    n_rows = -(-(b + 1) // 8) * 8
    cc = jnp.concatenate([c, c_ctx[None, :], jnp.zeros((n_rows - b - 1, d), c.dtype)], axis=0)
    mod_t = _ada(cc, w_ada, b_ada)

    vec = lambda t: t.reshape(depth, 1, -1).astype(F32)
    abr, abi, bbr, bbi = _s5_prep(s5_a_re, s5_a_im, s5_log_dt, s5_b_re, s5_b_im)
    bd, cd, a_row = _s5_matrices(abr, abi, bbr, bbi, s5_c_re, s5_c_im)
    na_bias = _na_bias_table(na_rel_bias, rows)
    w_in_b, w_glu_b, w_fn_b = w_in.astype(BF16), w_s5_glu.astype(BF16), w_fnet.astype(BF16)
    mlp_consts = (w_out.astype(BF16), vec(g_post_mix), vec(g_pre_mlp), vec(g_post_mlp),
                  w_mlp1.astype(BF16), w_mlp2.astype(BF16))
    g_pre, d_skip, b_fn = vec(g_pre_mix), vec(s5_d), vec(b_fnet)
    gq = jnp.tile(vec(g_q_attn), (1, 1, ATT_HEADS))
    gk = jnp.tile(vec(g_k_attn), (1, 1, ATT_KV_HEADS))
    rope_tabs = _rope_tables(n)
    lane = np.arange(GW)
    hm = jnp.asarray(np.where((lane[:, None] // HEAD_DIM) == (lane[None, :] // HEAD_DIM), 1.0 / HEAD_DIM, 0.0)
                     .astype(np.float32).astype(BF16))
    dft_n, ccs_n = _dft_matrices(n)
    dft_c, ccs_c = _dft_matrices(lc)

    tm_in = min(512, n)
    tm_c = min(256, lc)
    tc = math.gcd(64, math.gcd(n, lc))
    group = ATT_HEADS // ATT_KV_HEADS

    xc = ctx
    for l in range(depth):
        need_ctx = l < depth - 1
        qa, ka, va, us, nq, nk, nv, fn = _in_proj(x, mod_t, l, None, g_pre, w_in_b, gq, gk, hm, rope_tabs, tm_in)
        qac, kac, vac, usc, nqc, nkc, nvc, fnc = _in_proj(xc, mod_t, l, b, g_pre, w_in_b, gq, gk, hm, None, tm_c)

        oa = _gqa(qa, ka, va, kac, vac, group=group, tq=min(256, n))
        on = _na(nq, nk, nv, nkc, nvc, na_bias, l, rt=min(8, rows))
        od = _fnet(fn, dft_n, ccs_n, w_fn_b, b_fn, l, tk=min(512, n))
        obc, ob = _s5(usc, us, bd, cd, a_row, d_skip, w_glu_b, l, tc)
        x_new = _out_mlp(x, oa, ob, on, od, mod_t, l, None, *mlp_consts, tm_in)
        if need_ctx:
            oac = _gqa(qac, kac, vac, group=group, tq=tm_c)
            onc = _mha(nqc, nkc, nvc, tq=tm_c)
            odc = _fnet(fnc, dft_c, ccs_c, w_fn_b, b_fn, l, tk=tm_c)
            xc = _out_mlp(xc, oac, obc, onc, odc, mod_t, l, b, *mlp_consts, tm_c)
        x = x_new
    return x
```

```python
import functools
import math

import jax
import jax.numpy as jnp
import numpy as np
from jax import lax
from jax.experimental import pallas as pl
from jax.experimental.pallas import tpu as pltpu

F32 = jnp.float32
BF16 = jnp.bfloat16
HIGHEST = lax.Precision.HIGHEST

HEAD_DIM = 64
GRID_W = 64
N_MOD = 6
EPS = 1e-6
ATT_HEADS = 4
ATT_KV_HEADS = 2
ROPE_BASE = 10000.0
S5_GROUPS = 16
S5_GROUP_CH = 16
S5_STATE = 64
S5_MIN_DECAY = 1e-4
NA_HEADS = 4
NA_ROWS = 8
NA_COLS = 16
FN_GROUPS = 4
GW = 256
MASK_VALUE = -1e30
LOG2E = 1.4426950408889634
KEY_CHUNK = 1024
BF16_SUBLANES = 16

OFF_ATT_Q, OFF_ATT_K, OFF_ATT_V, OFF_S5 = 0, 256, 384, 512
OFF_NA_Q, OFF_NA_K, OFF_NA_V, OFF_FN, IN_WIDTH = 768, 1024, 1280, 1536, 1792

VMEM_LIMIT = 56 * 1024 * 1024


def _params(*sem):
    return pltpu.CompilerParams(dimension_semantics=sem, vmem_limit_bytes=VMEM_LIMIT)


def _const_spec(shape):
    return pl.BlockSpec(shape, lambda *_: (0,) * len(shape), pipeline_mode=pl.Buffered(1))


def _layer_spec(arr, *lead):
    k = len(lead)
    return pl.BlockSpec((None,) * k + arr.shape[k:], lambda *_: tuple(lead) + (0,) * (arr.ndim - k),
                        pipeline_mode=pl.Buffered(1))


def _rms(x, g):
    return x * lax.rsqrt(jnp.mean(x * x, axis=-1, keepdims=True) + EPS) * g


def _dot2(a, b):
    half = a.shape[0] // 2
    return jnp.concatenate([jnp.dot(a[:half], b, preferred_element_type=F32),
                            jnp.dot(a[half:], b, preferred_element_type=F32)], axis=0)


def _nt_dot(a, b):
    return lax.dot_general(a, b, (((1,), (1,)), ((), ())), preferred_element_type=F32)


def _ada_body(c_ref, w_ref, b_ref, o_ref):
    c = c_ref[...]
    s = c * jax.nn.sigmoid(c)
    o_ref[...] = jnp.dot(s, w_ref[...], precision=HIGHEST, preferred_element_type=F32) + b_ref[...]


def _ada(cc, w_ada, b_ada):
    depth, d, n_out = w_ada.shape
    rows = cc.shape[0]
    out = pl.pallas_call(
        _ada_body,
        grid=(depth, n_out // d),
        in_specs=[
            pl.BlockSpec((rows, d), lambda l, j: (0, 0)),
            pl.BlockSpec((None, d, d), lambda l, j: (l, 0, j)),
            pl.BlockSpec((None, 1, d), lambda l, j: (l, 0, j)),
        ],
        out_specs=pl.BlockSpec((None, None, rows, d), lambda l, j: (l, j, 0, 0)),
        out_shape=jax.ShapeDtypeStruct((depth, n_out // d, rows, d), F32),
        compiler_params=_params("parallel", "parallel"),
        name="ada",
    )(cc, w_ada, b_ada.reshape(depth, 1, n_out))
    return out.reshape(depth, n_out // d, rows, 1, d)


def _mod_spec(mod_t, layer, which, row):
    d = mod_t.shape[-1]
    return pl.BlockSpec((None, None, None, 1, d),
                        lambda bi, i: (layer, which, bi if row is None else row, 0, 0))


def _s5_prep_body(are_ref, aim_ref, ldt_ref, are_b_ref, aim_b_ref, ldt_b_ref, bre_ref, bim_ref,
                  abr_ref, abi_ref, bbr_ref, bbi_ref):
    def zoh(a_re, a_im, log_dt):
        lr = jnp.minimum(a_re, -S5_MIN_DECAY)
        dt = jnp.exp(log_dt)
        mag = jnp.exp(lr * dt)
        ab_r = mag * jnp.cos(a_im * dt)
        ab_i = mag * jnp.sin(a_im * dt)
        return lr, ab_r, ab_i

    _, ab_r, ab_i = zoh(are_ref[...], aim_ref[...], ldt_ref[...])
    abr_ref[...] = ab_r
    abi_ref[...] = ab_i
    lr, ab_r, ab_i = zoh(are_b_ref[...], aim_b_ref[...], ldt_b_ref[...])
    li = aim_b_ref[...]
    nr, ni = ab_r - 1.0, ab_i
    den = lr * lr + li * li
    kr = (nr * lr + ni * li) / den
    ki = (ni * lr - nr * li) / den
    br, bi = bre_ref[...], bim_ref[...]
    bbr_ref[...] = kr * br - ki * bi
    bbi_ref[...] = kr * bi + ki * br


def _s5_prep(a_re, a_im, log_dt, b_re, b_im):
    lead = a_re.shape[:3]
    r = lead[0] * lead[1] * lead[2]
    p, h = S5_STATE, S5_GROUP_CH
    a2 = lambda t: t.reshape(r, p)
    ab = lambda t: jnp.broadcast_to(t.reshape(r, p, 1), (r, p, h)).reshape(r, p * h)
    ldt = jnp.broadcast_to(log_dt.reshape(r, 1), (r, p))
    ldt_b = jnp.broadcast_to(log_dt.reshape(r, 1), (r, p * h))
    small = jax.ShapeDtypeStruct((r, p), F32)
    big = jax.ShapeDtypeStruct((r, p * h), F32)
    abr, abi, bbr, bbi = pl.pallas_call(
        _s5_prep_body, out_shape=(small, small, big, big), name="s5_prep",
    )(a2(a_re), a2(a_im), ldt, ab(a_re), ab(a_im), ldt_b, b_re.reshape(r, p * h), b_im.reshape(r, p * h))
    return (abr.reshape(*lead, p), abi.reshape(*lead, p),
            bbr.reshape(*lead, p, h), bbi.reshape(*lead, p, h))


def _in_proj_body(*refs, rope, sub):
    if rope:
        (x_ref, sh_ref, sc_ref, g_ref, w_ref, gq_ref, gk_ref, hm_ref, cos_ref, sa_ref, sb_ref,
         qa_ref, ka_ref, va_ref, us_ref, nq_ref, nk_ref, nv_ref, fn_ref) = refs
    else:
        (x_ref, sh_ref, sc_ref, g_ref, w_ref, gq_ref, gk_ref, hm_ref,
         qa_ref, ka_ref, va_ref, us_ref, nq_ref, nk_ref, nv_ref, fn_ref) = refs
    hm = hm_ref[...]
    kw = OFF_ATT_V - OFF_ATT_K
    scale = HEAD_DIM ** -0.5 * LOG2E
    lane = lax.broadcasted_iota(jnp.int32, (sub, HEAD_DIM), 1)
    one_col = jnp.where(lane == 0, 1.0, 0.0)

    def head_norm(t, g, avg):
        ms = jnp.dot((t * t).astype(BF16), avg, preferred_element_type=F32)
        return t * lax.rsqrt(ms + EPS) * g

    for r0 in range(0, x_ref.shape[0], sub):
        rs = slice(r0, r0 + sub)
        h = _rms(x_ref[rs, :], g_ref[...]) * (1.0 + sc_ref[...]) + sh_ref[...]
        hb = h.astype(BF16)
        z_lo = jnp.dot(hb, w_ref[:, :OFF_NA_K], preferred_element_type=F32)
        z_hi = jnp.dot(hb, w_ref[:, OFF_NA_K:], preferred_element_type=F32)
        q = head_norm(z_lo[:, OFF_ATT_Q:OFF_ATT_K], gq_ref[...], hm)
        k = head_norm(z_lo[:, OFF_ATT_K:OFF_ATT_V], gk_ref[...], hm[:kw, :kw])
        if rope:
            def rot(t):
                w = t.shape[-1]
                half = HEAD_DIM // 4
                return (t * cos_ref[rs, :w] + pltpu.roll(t, w - half, 1) * sa_ref[rs, :w]
                        + pltpu.roll(t, half, 1) * sb_ref[rs, :w])
            q, k = rot(q), rot(k)
        qa_ref[rs, :] = (q * scale).astype(qa_ref.dtype)
        ka_ref[rs, :] = k.astype(ka_ref.dtype)
        v = z_lo[:, OFF_ATT_V:OFF_S5]
        v_ext = jnp.concatenate(
            sum(([v[:, HEAD_DIM * i:HEAD_DIM * (i + 1)], one_col] for i in range(ATT_KV_HEADS)), []), axis=-1)
        va_ref[:, rs] = v_ext.T.astype(va_ref.dtype)
        us_ref[rs, :] = z_lo[:, OFF_S5:OFF_NA_Q].astype(us_ref.dtype)
        nq_ref[rs, :] = (z_lo[:, OFF_NA_Q:OFF_NA_K] * scale).astype(nq_ref.dtype)
        nk_ref[rs, :] = z_hi[:, :GW].astype(nk_ref.dtype)
        nv_ref[rs, :] = z_hi[:, GW:2 * GW].astype(nv_ref.dtype)
        fn_ref[rs, :] = z_hi[:, 2 * GW:].astype(fn_ref.dtype)


def _in_proj(x, mod_t, layer, mod_row, g_pre, w_in, gq, gk, hm, rope_tabs, tm):
    b, n, d = x.shape
    rope = rope_tabs is not None
    row = lambda w: pl.BlockSpec((None, tm, w), lambda bi, i: (bi, i, 0))
    in_specs = [row(d), _mod_spec(mod_t, layer, 0, mod_row), _mod_spec(mod_t, layer, 1, mod_row)]
    in_specs += [_layer_spec(t, layer) for t in (g_pre, w_in, gq, gk)] + [_const_spec(hm.shape)]
    args = [x, mod_t, mod_t, g_pre, w_in, gq, gk, hm]
    if rope:
        in_specs += [pl.BlockSpec((tm, GW), lambda bi, i: (i, 0))] * 3
        args += list(rope_tabs)
    kvw = ATT_KV_HEADS * HEAD_DIM
    widths = [GW, kvw, None, GW, GW, GW, GW, GW]
    out_specs = [pl.BlockSpec((None, 2 * kvw, tm), lambda bi, i: (bi, 0, i)) if w is None else row(w) for w in widths]
    out_shape = [jax.ShapeDtypeStruct((b, 2 * kvw, n) if w is None else (b, n, w), BF16) for w in widths]
    return pl.pallas_call(
        functools.partial(_in_proj_body, rope=rope, sub=min(256, tm)),
        grid=(b, n // tm), in_specs=in_specs, out_specs=out_specs, out_shape=out_shape,
        compiler_params=_params("parallel", "parallel"), name="in_proj_rope" if rope else "in_proj",
    )(*args)


def _col_max(s):
    for rows in (256, 64):
        if s.shape[0] > rows and s.shape[0] % rows == 0:
            s = jnp.max(s.reshape(s.shape[0] // rows, rows, s.shape[1]), axis=0)
    return jnp.max(s, axis=0, keepdims=True)


def _gqa_body(*refs, n_heads, group, two_sets, sub):
    if two_sets:
        q_ref, k1_ref, v1t_ref, k2_ref, v2t_ref, o_ref = refs
    else:
        q_ref, k1_ref, v1t_ref, o_ref = refs
    key_sets = [(k1_ref, v1t_ref)] + ([(k2_ref, v2t_ref)] if two_sets else [])

    def scores(unit):
        r0, h = unit
        ks = slice(HEAD_DIM * (h // group), HEAD_DIM * (h // group + 1))
        qh = q_ref[r0:r0 + sub, HEAD_DIM * h:HEAD_DIM * (h + 1)]
        s = [_nt_dot(k_ref[:, ks], qh).astype(BF16) for k_ref, _ in key_sets]
        m = functools.reduce(jnp.maximum, [_col_max(t) for t in s])
        return s, m

    def weighted_values(h, s, m):
        vr = slice(2 * HEAD_DIM * (h // group), 2 * HEAD_DIM * (h // group + 1))
        o = None
        for t, (_, vt_ref) in zip(s, key_sets):
            for c0 in range(0, t.shape[0], KEY_CHUNK):
                p = jnp.exp2(t[c0:c0 + KEY_CHUNK] - m)
                part = jnp.dot(vt_ref[vr, c0:c0 + KEY_CHUNK], p, preferred_element_type=F32)
                o = part if o is None else o + part
        return o[:HEAD_DIM] / o[HEAD_DIM:HEAD_DIM + 1]

    units = [(r0, h) for r0 in range(0, q_ref.shape[0], sub) for h in range(n_heads)]
    outs = []
    nxt = scores(units[0])
    for i, (r0, h) in enumerate(units):
        cur, nxt = nxt, (scores(units[i + 1]) if i + 1 < len(units) else None)
        outs.append(weighted_values(h, *cur))
        if h == n_heads - 1:
            o_ref[r0:r0 + sub, :] = jnp.concatenate(outs, axis=0).T.astype(o_ref.dtype)
            outs = []


def _gqa(q, k1, v1t, k2=None, v2t=None, *, group, tq):
    b, nq, qw = q.shape
    two_sets = k2 is not None
    full = lambda t: pl.BlockSpec((None,) + t.shape[1:], lambda bi, i: (bi, 0, 0))
    args = [q, k1, v1t] + ([k2, v2t] if two_sets else [])
    in_specs = [pl.BlockSpec((None, tq, qw), lambda bi, i: (bi, i, 0))] + [full(t) for t in args[1:]]
    return pl.pallas_call(
        functools.partial(_gqa_body, n_heads=qw // HEAD_DIM, group=group, two_sets=two_sets, sub=min(256, tq)),
        grid=(b, nq // tq), in_specs=in_specs,
        out_specs=pl.BlockSpec((None, tq, qw), lambda bi, i: (bi, i, 0)),
        out_shape=jax.ShapeDtypeStruct((b, nq, qw), BF16),
        compiler_params=_params("parallel", "parallel"), name="gqa2" if two_sets else "gqa1",
    )(*args)


def _mha_body(q_ref, k_ref, v_ref, o_ref, *, n_heads):
    for h in range(n_heads):
        hs = slice(HEAD_DIM * h, HEAD_DIM * (h + 1))
        s = _nt_dot(q_ref[:, hs], k_ref[:, hs])
        p = jnp.exp2(s - jnp.max(s, axis=-1, keepdims=True))
        o = jnp.dot(p.astype(BF16), v_ref[:, hs], preferred_element_type=F32)
        o_ref[:, hs] = (o / jnp.sum(p, axis=-1, keepdims=True)).astype(o_ref.dtype)


def _mha(q, k, v, *, tq):
    b, nq, qw = q.shape
    full = lambda t: pl.BlockSpec((None,) + t.shape[1:], lambda bi, i: (bi, 0, 0))
    return pl.pallas_call(
        functools.partial(_mha_body, n_heads=qw // HEAD_DIM),
        grid=(b, nq // tq),
        in_specs=[pl.BlockSpec((None, tq, qw), lambda bi, i: (bi, i, 0)), full(k), full(v)],
        out_specs=pl.BlockSpec((None, tq, qw), lambda bi, i: (bi, i, 0)),
        out_shape=jax.ShapeDtypeStruct((b, nq, qw), BF16),
        compiler_params=_params("parallel", "parallel"), name="mha",
    )(q, k, v)


def _na_body(q_ref, k_ref, v_ref, kc_ref, vc_ref, bias_ref, o_ref, *, rt, k_r, rows):
    i = pl.program_id(1)
    kc = kc_ref[...]
    vc = vc_ref[...]
    lane_head = lax.broadcasted_iota(jnp.int32, (GRID_W, GW), 1) // HEAD_DIM
    def scores(j):
        r = i * rt + j
        rs = jnp.clip(r - k_r // 2, 0, rows - k_r)
        start = pl.multiple_of(rs * GRID_W, GRID_W)
        q = q_ref[j * GRID_W:(j + 1) * GRID_W, :]
        q4 = jnp.concatenate([jnp.where(lane_head == h, q, jnp.zeros_like(q)) for h in range(NA_HEADS)], axis=0)
        s_loc = (_nt_dot(q4, k_ref[pl.ds(start, k_r * GRID_W), :]) + bias_ref[r - rs]).astype(BF16)
        s_ctx = _nt_dot(q4, kc).astype(BF16)
        return start, s_loc, s_ctx

    def weighted_values(j, start, s_loc, s_ctx):
        m = jnp.maximum(jnp.max(s_loc, axis=-1, keepdims=True), jnp.max(s_ctx, axis=-1, keepdims=True))
        p_loc = jnp.exp2(s_loc - m)
        p_ctx = jnp.exp2(s_ctx - m)
        l = (jnp.sum(p_loc.astype(F32), axis=-1, keepdims=True)
             + jnp.sum(p_ctx.astype(F32), axis=-1, keepdims=True))
        o4 = (jnp.dot(p_loc, v_ref[pl.ds(start, k_r * GRID_W), :], preferred_element_type=F32)
              + jnp.dot(p_ctx, vc, preferred_element_type=F32)) / l
        o = jnp.zeros((GRID_W, GW), F32)
        for h in range(NA_HEADS):
            o = o + jnp.where(lane_head == h, o4[h * GRID_W:(h + 1) * GRID_W, :], 0.0)
        o_ref[j * GRID_W:(j + 1) * GRID_W, :] = o.astype(o_ref.dtype)

    nxt = scores(0)
    for j in range(rt):
        cur, nxt = nxt, (scores(j + 1) if j + 1 < rt else None)
        weighted_values(j, *cur)


def _na(q, k, v, kc, vc, bias, layer, rt):
    b, n, _ = q.shape
    rows = n // GRID_W
    k_r = bias.shape[1]
    full = lambda t: pl.BlockSpec((None,) + t.shape[1:], lambda bi, i: (bi, 0, 0))
    return pl.pallas_call(
        functools.partial(_na_body, rt=rt, k_r=k_r, rows=rows),
        grid=(b, rows // rt),
        in_specs=[pl.BlockSpec((None, rt * GRID_W, GW), lambda bi, i: (bi, i, 0)),
                  full(k), full(v), full(kc), full(vc), _layer_spec(bias, layer)],
        out_specs=pl.BlockSpec((None, rt * GRID_W, GW), lambda bi, i: (bi, i, 0)),
        out_shape=jax.ShapeDtypeStruct((b, n, GW), BF16),
        compiler_params=_params("parallel", "parallel"), name="na",
    )(q, k, v, kc, vc, bias)


def _na_bias_table(rel_bias, rows):
    k_r = min(NA_ROWS, rows)
    cols = np.arange(GRID_W)
    col_start = np.clip(cols - NA_COLS // 2, 0, GRID_W - NA_COLS)
    inside = (cols[None, :] >= col_start[:, None]) & (cols[None, :] < col_start[:, None] + NA_COLS)
    rel_c = cols[None, :] - cols[:, None] + (NA_COLS - 1)
    rel_r = np.arange(k_r)[None, :] - np.arange(k_r)[:, None] + (NA_ROWS - 1)
    pick_r = (rel_r[:, :, None] == np.arange(2 * NA_ROWS - 1)).astype(np.float32)
    pick_c = ((rel_c[:, :, None] == np.arange(2 * NA_COLS - 1)) & inside[:, :, None]).astype(np.float32)
    t = jnp.einsum("cai,lhim->lcham", pick_r, rel_bias.astype(F32), precision=HIGHEST)
    t = jnp.einsum("lcham,jkm->lchjak", t, pick_c, precision=HIGHEST)
    t = t + np.where(inside, 0.0, MASK_VALUE).astype(np.float32)[None, None, None, :, None, :]
    return (t * LOG2E).reshape(rel_bias.shape[0], k_r, NA_HEADS * GRID_W, k_r * GRID_W)


def _s5_scan(hs_ref, st_ref, a_ref, *, tc, nb, n_state, col_w, reverse):
    for c0 in range(0, n_state, col_w):
        re = slice(c0, c0 + col_w)
        im = slice(n_state + c0, n_state + c0 + col_w)
        ar = jnp.broadcast_to(a_ref[:, re], (nb, col_w))
        ai = jnp.broadcast_to(a_ref[:, im], (nb, col_w))

        hr, hi = st_ref[:, re], st_ref[:, im]
        for t in range(tc):
            rows = slice((tc - 1 - t if reverse else t) * nb, (tc - t if reverse else t + 1) * nb)
            hr, hi = ar * hr - ai * hi + hs_ref[rows, re], ar * hi + ai * hr + hs_ref[rows, im]
            hs_ref[rows, re] = hr
            hs_ref[rows, im] = hi
        st_ref[:, re] = hr
        st_ref[:, im] = hi


def _s5_fwd_body(uc_ref, ul_ref, perm_ref, bd_ref, cd_ref, a_ref, ut_ref, yf_ref, hs0_ref, hs1_ref, st_ref,
                 *, n_c, scan):
    j = pl.program_id(0)

    @pl.when(j == 0)
    def _():
        st_ref[...] = jnp.zeros_like(st_ref)
        hs1_ref[...] = jnp.zeros_like(hs1_ref)

    def step(cur, prev):
        nb, tc, w = uc_ref.shape
        ts = perm_ref.shape[0] // nb
        u_bm = jnp.where(j < n_c, uc_ref[...], ul_ref[...])
        ut = jnp.concatenate(
            [jnp.dot(perm_ref[...], u_bm[:, s:s + ts, :].reshape(nb * ts, w), preferred_element_type=F32)
             for s in range(0, tc, ts)], axis=0).astype(ut_ref.dtype)
        ut_ref[...] = ut
        cur[...] = _dot2(ut, bd_ref[...])
        scan(prev, st_ref, a_ref, reverse=False)
        yf_ref[...] = _dot2(prev[...].astype(BF16), cd_ref[...])

    pl.when(j % 2 == 0)(lambda: step(hs0_ref, hs1_ref))
    pl.when(j % 2 == 1)(lambda: step(hs1_ref, hs0_ref))


def _s5_bwd_body(ut_ref, utp_ref, yf_ref, permt_ref, bd_ref, cd_ref, a_ref, dsk_ref, wglu_ref, oc_ref, ol_ref,
                 hs0_ref, hs1_ref, st_ref, *, n_c, scan):
    j = pl.program_id(0)

    @pl.when(j == 0)
    def _():
        st_ref[...] = jnp.zeros_like(st_ref)
        hs1_ref[...] = jnp.zeros_like(hs1_ref)

    def step(cur, prev):
        cur[...] = _dot2(ut_ref[...], bd_ref[...])
        scan(prev, st_ref, a_ref, reverse=True)
        t = (dsk_ref[...] * utp_ref[...].astype(F32) + yf_ref[...]
             + _dot2(prev[...].astype(BF16), cd_ref[...]))
        g = jax.nn.gelu(t)
        ob = (g * jax.nn.sigmoid(_dot2(g.astype(BF16), wglu_ref[...]))).astype(BF16)
        nb, tc, w = oc_ref.shape
        rows = permt_ref.shape[0]
        groups = [jnp.dot(permt_ref[...], ob[r0:r0 + rows], preferred_element_type=F32).astype(BF16)
                  .reshape(nb, rows // nb, w) for r0 in range(0, tc * nb, rows)]

        def write(o_ref):
            for s, grp in enumerate(groups):
                o_ref[:, s * grp.shape[1]:(s + 1) * grp.shape[1], :] = grp

        pl.when(j <= n_c)(lambda: write(oc_ref))
        pl.when(j > n_c)(lambda: write(ol_ref))

    pl.when(j % 2 == 0)(lambda: step(hs0_ref, hs1_ref))
    pl.when(j % 2 == 1)(lambda: step(hs1_ref, hs0_ref))


def _s5(usc, us, bd, cd, a_row, dsk, wglu, layer, tc):
    nb, lc, _ = usc.shape
    n = us.shape[1]
    n_c, n_l = lc // tc, n // tc
    n_all = n_c + n_l
    n_state2 = bd.shape[-1]
    rows = tc * nb
    ts = math.gcd(tc, BF16_SUBLANES)
    r = np.arange(ts * nb)
    perm = np.zeros((ts * nb, ts * nb), np.float32)
    perm[r, (r % nb) * ts + r // nb] = 1.0
    scan = functools.partial(_s5_scan, tc=tc, nb=nb, n_state=n_state2 // 2, col_w=512)
    scratch = [pltpu.VMEM((rows, n_state2), F32), pltpu.VMEM((nb, n_state2), F32)]
    seg = lambda idx: pl.BlockSpec((nb, tc, GW), lambda j: (0, idx(j), 0))
    tm_rows = lambda idx: pl.BlockSpec((rows, GW), lambda j: (idx(j), 0))

    ut, yf = pl.pallas_call(
        functools.partial(_s5_fwd_body, n_c=n_c, scan=scan), grid=(n_all + 1,),
        in_specs=[seg(lambda j: jnp.minimum(j, n_c - 1)), seg(lambda j: jnp.clip(j - n_c, 0, n_l - 1)),
                  _const_spec(perm.shape)] + [_layer_spec(t, layer, 0) for t in (bd, cd, a_row)],
        out_specs=[tm_rows(lambda j: jnp.minimum(j, n_all - 1)), tm_rows(lambda j: jnp.maximum(j - 1, 0))],
        out_shape=[jax.ShapeDtypeStruct((n_all * rows, GW), BF16), jax.ShapeDtypeStruct((n_all * rows, GW), F32)],
        scratch_shapes=[scratch[0], scratch[0], scratch[1]],
        compiler_params=_params("arbitrary"), name="s5_fwd",
    )(usc, us, jnp.asarray(perm.astype(BF16)), bd, cd, a_row)

    def blk(j):
        j = jnp.clip(j, 0, n_all - 1)
        return jnp.where(j < n_c, n_c - 1 - j, n_c + n_all - 1 - j)

    prev = lambda j: jnp.maximum(j - 1, 0)
    return pl.pallas_call(
        functools.partial(_s5_bwd_body, n_c=n_c, scan=scan), grid=(n_all + 1,),
        in_specs=[tm_rows(blk), tm_rows(lambda j: blk(prev(j))), tm_rows(lambda j: blk(prev(j))),
                  _const_spec(perm.shape)] + [_layer_spec(t, layer, 1) for t in (bd, cd, a_row)]
        + [_layer_spec(dsk, layer), _layer_spec(wglu, layer)],
        out_specs=[seg(lambda j: jnp.maximum(n_c - 1 - prev(j), 0)),
                   seg(lambda j: n_l - 1 - jnp.maximum(prev(j) - n_c, 0))],
        out_shape=[jax.ShapeDtypeStruct(usc.shape, BF16), jax.ShapeDtypeStruct(us.shape, BF16)],
        scratch_shapes=[scratch[0], scratch[0], scratch[1]], compiler_params=_params("arbitrary"), name="s5_bwd",
    )(ut, ut, yf, jnp.asarray(perm.T.astype(BF16)), bd, cd, a_row, dsk, wglu)


def _s5_matrices(abr, abi, bbr, bbi, c_re, c_im):
    g, p, h = S5_GROUPS, S5_STATE, S5_GROUP_CH
    lead = abr.shape[:2]
    eye = np.eye(g, dtype=np.float32)[None, None, :, None, :, None]
    spread = lambda t: (jnp.swapaxes(t, -1, -2)[:, :, :, :, None, :] * eye).reshape(
        *lead, g * t.shape[-1], g * t.shape[-2])
    bd = jnp.concatenate([spread(bbr), spread(bbi)], axis=-1)
    cd = jnp.concatenate([spread(c_re.astype(F32)), spread(-c_im.astype(F32))], axis=-2)
    a_row = jnp.concatenate([abr.reshape(*lead, 1, g * p), abi.reshape(*lead, 1, g * p)], axis=-1)
    return bd.astype(BF16), cd.astype(BF16), a_row


def _fnet_body(u_ref, mirror_ref, dft_ref, ccs_ref, w_ref, b_ref, o_ref, ue_ref):
    h = u_ref.shape[0] // 2

    @pl.when(pl.program_id(1) == 0)
    def _():
        uf = jnp.dot(mirror_ref[...], u_ref[h:2 * h, :], preferred_element_type=F32).astype(BF16)
        t = jnp.dot(u_ref[0:h, :], ccs_ref[...], preferred_element_type=F32)
        tf = jnp.dot(uf, ccs_ref[...], preferred_element_type=F32)
        row0 = lax.broadcasted_iota(jnp.int32, (h, GW), 0) == 0
        ue_ref[0:h, :] = (t[:, :GW] + jnp.where(row0, 0.0, tf[:, :GW])).astype(BF16)
        ue_ref[h:2 * h, :] = jnp.where(row0, tf[:, :GW], t[:, GW:] - tf[:, GW:]).astype(BF16)

    tk = o_ref.shape[0]
    rows = pl.ds(pl.multiple_of(pl.program_id(1) * tk, tk), tk)
    f = _dot2(dft_ref[rows, :], ue_ref[...])
    o = jnp.dot(f.astype(BF16), w_ref[...], preferred_element_type=F32) + b_ref[...]
    o_ref[...] = o.astype(o_ref.dtype)


def _fnet(u, dft, ccs, w, bias, layer, tk):
    b, n, _ = u.shape
    h = n // 2
    mirror = np.zeros((h, h), np.float32)
    mirror[np.arange(h), (h - np.arange(h)) % h] = 1.0
    return pl.pallas_call(
        _fnet_body, grid=(b, n // tk),
        in_specs=[pl.BlockSpec((None, n, GW), lambda bi, k: (bi, 0, 0)), _const_spec(mirror.shape),
                  _const_spec(dft.shape),
                  _const_spec(ccs.shape), _layer_spec(w, layer), _layer_spec(bias, layer)],
        out_specs=pl.BlockSpec((None, tk, GW), lambda bi, k: (bi, k, 0)),
        out_shape=jax.ShapeDtypeStruct((b, n, GW), BF16),
        scratch_shapes=[pltpu.VMEM((n, GW), BF16)],
        compiler_params=_params("parallel", "arbitrary"), name="fnet",
    )(u, jnp.asarray(mirror.astype(BF16)), dft, ccs, w, bias)


def _dft_matrices(n):
    ch = GW // FN_GROUPS
    h = n // 2
    k = np.arange(n)
    ang = (2.0 * np.pi / n) * ((k[:, None] * np.arange(h)[None, :]) % n)
    m = -np.sin(ang)
    m[:, 0] = 1.0 - 2.0 * (k % 2)
    dft = np.concatenate([np.cos(ang), m], axis=1)
    c = np.arange(GW)
    same = (c[:, None] // ch) == (c[None, :] // ch)
    angc = (2.0 * np.pi / ch) * (((c[:, None] % ch) * (c[None, :] % ch)) % ch)
    norm = 1.0 / math.sqrt(n * ch)
    ccs = np.concatenate([np.where(same, np.cos(angc), 0.0), np.where(same, np.sin(angc), 0.0)], axis=1) * norm
    return (jnp.asarray(dft.astype(np.float32)).astype(BF16),
            jnp.asarray(ccs.astype(np.float32)).astype(BF16))


def _out_mlp_body(x_ref, oa_ref, ob_ref, on_ref, od_ref, g1_ref, sh2_ref, sc2_ref, g2_ref,
                  wout_ref, gpm_ref, gpre_ref, gpost_ref, w1_ref, w2_ref, o_ref, *, ff_chunk):
    cat = jnp.concatenate([oa_ref[...], ob_ref[...], on_ref[...], od_ref[...]], axis=-1)
    y = jnp.dot(cat, wout_ref[...], preferred_element_type=F32)
    x1 = x_ref[...] + g1_ref[...] * _rms(y, gpm_ref[...])
    h = (_rms(x1, gpre_ref[...]) * (1.0 + sc2_ref[...]) + sh2_ref[...]).astype(BF16)
    d_ff = w1_ref.shape[1]
    m = jnp.zeros(x1.shape, F32)
    for c0 in range(0, d_ff, ff_chunk):
        a = jnp.maximum(jnp.dot(h, w1_ref[:, c0:c0 + ff_chunk], preferred_element_type=F32), 0.0)
        m = m + jnp.dot((a * a).astype(BF16), w2_ref[c0:c0 + ff_chunk, :], preferred_element_type=F32)
    o_ref[...] = x1 + g2_ref[...] * _rms(m, gpost_ref[...])


def _out_mlp(x, oa, ob, on, od, mod_t, layer, mod_row, wout, gpm, gpre, gpost, w1, w2, tm):
    b, n, d = x.shape
    row = lambda w: pl.BlockSpec((None, tm, w), lambda bi, i: (bi, i, 0))
    consts = [wout, gpm, gpre, gpost, w1, w2]
    return pl.pallas_call(
        functools.partial(_out_mlp_body, ff_chunk=1024),
        grid=(b, n // tm),
        in_specs=[row(d), row(GW), row(GW), row(GW), row(GW)]
        + [_mod_spec(mod_t, layer, which, mod_row) for which in (2, 3, 4, 5)]
        + [_layer_spec(t, layer) for t in consts],
        out_specs=row(d), out_shape=jax.ShapeDtypeStruct((b, n, d), F32),
        compiler_params=_params("parallel", "parallel"), name="out_mlp",
    )(x, oa, ob, on, od, mod_t, mod_t, mod_t, mod_t, *consts)


def _rope_tables(n):
    half = HEAD_DIM // 4
    pos = np.arange(n)
    jj = np.arange(GW) % HEAD_DIM
    inv = ROPE_BASE ** (-(jj % half) / half)
    p = np.where((jj // (2 * half) == 0)[None, :], (pos // GRID_W)[:, None], (pos % GRID_W)[:, None])
    ang = p * inv[None, :]
    first = ((jj % (2 * half)) < half)[None, :]
    sin = np.sin(ang)
    tabs = (np.cos(ang), np.where(first, -sin, 0.0), np.where(first, 0.0, sin))
    return tuple(jnp.asarray(t.astype(np.float32)) for t in tabs)


def kernel(x, c, ctx, c_ctx, w_ada, b_ada, g_pre_mix, g_post_mix, g_pre_mlp, g_post_mlp, w_in, g_q_attn, g_k_attn,
           s5_a_re, s5_a_im, s5_log_dt, s5_b_re, s5_b_im, s5_c_re, s5_c_im, s5_d, w_s5_glu, na_rel_bias,
           w_fnet, b_fnet, w_out, w_mlp1, w_mlp2):
    b, n, d = x.shape
    lc = ctx.shape[1]
    depth = w_ada.shape[0]
    rows = n // GRID_W

    n_rows = -(-(b + 1) // 8) * 8
    cc = jnp.concatenate([c, c_ctx[None, :], jnp.zeros((n_rows - b - 1, d), c.dtype)], axis=0)
    mod_t = _ada(cc, w_ada, b_ada)

    vec = lambda t: t.reshape(depth, 1, -1).astype(F32)
    abr, abi, bbr, bbi = _s5_prep(s5_a_re, s5_a_im, s5_log_dt, s5_b_re, s5_b_im)
    bd, cd, a_row = _s5_matrices(abr, abi, bbr, bbi, s5_c_re, s5_c_im)
    na_bias = _na_bias_table(na_rel_bias, rows)
    w_in_b, w_glu_b, w_fn_b = w_in.astype(BF16), w_s5_glu.astype(BF16), w_fnet.astype(BF16)
    mlp_consts = (w_out.astype(BF16), vec(g_post_mix), vec(g_pre_mlp), vec(g_post_mlp),
                  w_mlp1.astype(BF16), w_mlp2.astype(BF16))
    g_pre, d_skip, b_fn = vec(g_pre_mix), vec(s5_d), vec(b_fnet)
    gq = jnp.tile(vec(g_q_attn), (1, 1, ATT_HEADS))
    gk = jnp.tile(vec(g_k_attn), (1, 1, ATT_KV_HEADS))
    rope_tabs = _rope_tables(n)
    lane = np.arange(GW)
    hm = jnp.asarray(np.where((lane[:, None] // HEAD_DIM) == (lane[None, :] // HEAD_DIM), 1.0 / HEAD_DIM, 0.0)
                     .astype(np.float32).astype(BF16))
    dft_n, ccs_n = _dft_matrices(n)
    dft_c, ccs_c = _dft_matrices(lc)

    tm_in = min(512, n)
    tm_c = min(256, lc)
    tc = math.gcd(64, math.gcd(n, lc))
    group = ATT_HEADS // ATT_KV_HEADS

    xc = ctx
    for l in range(depth):
        need_ctx = l < depth - 1
        qa, ka, va, us, nq, nk, nv, fn = _in_proj(x, mod_t, l, None, g_pre, w_in_b, gq, gk, hm, rope_tabs, tm_in)
        qac, kac, vac, usc, nqc, nkc, nvc, fnc = _in_proj(xc, mod_t, l, b, g_pre, w_in_b, gq, gk, hm, None, tm_c)

        oa = _gqa(qa, ka, va, kac, vac, group=group, tq=min(1024, n))
        on = _na(nq, nk, nv, nkc, nvc, na_bias, l, rt=min(8, rows))
        od = _fnet(fn, dft_n, ccs_n, w_fn_b, b_fn, l, tk=min(512, n))
        obc, ob = _s5(usc, us, bd, cd, a_row, d_skip, w_glu_b, l, tc)
        x_new = _out_mlp(x, oa, ob, on, od, mod_t, l, None, *mlp_consts, tm_in)
        if need_ctx:
            oac = _gqa(qac, kac, vac, group=group, tq=tm_c)
            onc = _mha(nqc, nkc, nvc, tq=tm_c)
            odc = _fnet(fnc, dft_c, ccs_c, w_fn_b, b_fn, l, tk=tm_c)
            xc = _out_mlp(xc, oac, obc, onc, odc, mod_t, l, b, *mlp_consts, tm_c)
        x = x_new
    return x
```

```python
import functools
import math

import jax
import jax.numpy as jnp
import numpy as np
from jax import lax
from jax.experimental import pallas as pl
from jax.experimental.pallas import tpu as pltpu

F32 = jnp.float32
BF16 = jnp.bfloat16
HIGHEST = lax.Precision.HIGHEST

HEAD_DIM = 64
GRID_W = 64
N_MOD = 6
EPS = 1e-6
ATT_HEADS = 4
ATT_KV_HEADS = 2
ROPE_BASE = 10000.0
S5_GROUPS = 16
S5_GROUP_CH = 16
S5_STATE = 64
S5_MIN_DECAY = 1e-4
NA_HEADS = 4
NA_ROWS = 8
NA_COLS = 16
FN_GROUPS = 4
GW = 256
MASK_VALUE = -1e30
LOG2E = 1.4426950408889634
KEY_CHUNK = 1024
BF16_SUBLANES = 16

OFF_ATT_Q, OFF_ATT_K, OFF_ATT_V, OFF_S5 = 0, 256, 384, 512
OFF_NA_Q, OFF_NA_K, OFF_NA_V, OFF_FN, IN_WIDTH = 768, 1024, 1280, 1536, 1792

VMEM_LIMIT = 56 * 1024 * 1024


def _params(*sem):
    return pltpu.CompilerParams(dimension_semantics=sem, vmem_limit_bytes=VMEM_LIMIT)


def _const_spec(shape):
    return pl.BlockSpec(shape, lambda *_: (0,) * len(shape), pipeline_mode=pl.Buffered(1))


def _layer_spec(arr, *lead):
    k = len(lead)
    return pl.BlockSpec((None,) * k + arr.shape[k:], lambda *_: tuple(lead) + (0,) * (arr.ndim - k),
                        pipeline_mode=pl.Buffered(1))


def _rms(x, g):
    return x * lax.rsqrt(jnp.mean(x * x, axis=-1, keepdims=True) + EPS) * g


def _dot2(a, b):
    half = a.shape[0] // 2
    return jnp.concatenate([jnp.dot(a[:half], b, preferred_element_type=F32),
                            jnp.dot(a[half:], b, preferred_element_type=F32)], axis=0)


def _nt_dot(a, b):
    return lax.dot_general(a, b, (((1,), (1,)), ((), ())), preferred_element_type=F32)


def _ada_body(c_ref, w_ref, b_ref, o_ref):
    c = c_ref[...]
    s = c * jax.nn.sigmoid(c)
    o_ref[...] = jnp.dot(s, w_ref[...], precision=HIGHEST, preferred_element_type=F32) + b_ref[...]


def _ada(cc, w_ada, b_ada):
    depth, d, n_out = w_ada.shape
    rows = cc.shape[0]
    out = pl.pallas_call(
        _ada_body,
        grid=(depth, n_out // d),
        in_specs=[
            pl.BlockSpec((rows, d), lambda l, j: (0, 0)),
            pl.BlockSpec((None, d, d), lambda l, j: (l, 0, j)),
            pl.BlockSpec((None, 1, d), lambda l, j: (l, 0, j)),
        ],
        out_specs=pl.BlockSpec((None, None, rows, d), lambda l, j: (l, j, 0, 0)),
        out_shape=jax.ShapeDtypeStruct((depth, n_out // d, rows, d), F32),
        compiler_params=_params("parallel", "parallel"),
        name="ada",
    )(cc, w_ada, b_ada.reshape(depth, 1, n_out))
    return out.reshape(depth, n_out // d, rows, 1, d)


def _mod_spec(mod_t, layer, which, row):
    d = mod_t.shape[-1]
    return pl.BlockSpec((None, None, None, 1, d),
                        lambda bi, i: (layer, which, bi if row is None else row, 0, 0))


def _s5_prep_body(are_ref, aim_ref, ldt_ref, are_b_ref, aim_b_ref, ldt_b_ref, bre_ref, bim_ref,
                  abr_ref, abi_ref, bbr_ref, bbi_ref):
    def zoh(a_re, a_im, log_dt):
        lr = jnp.minimum(a_re, -S5_MIN_DECAY)
        dt = jnp.exp(log_dt)
        mag = jnp.exp(lr * dt)
        ab_r = mag * jnp.cos(a_im * dt)
        ab_i = mag * jnp.sin(a_im * dt)
        return lr, ab_r, ab_i

    _, ab_r, ab_i = zoh(are_ref[...], aim_ref[...], ldt_ref[...])
    abr_ref[...] = ab_r
    abi_ref[...] = ab_i
    lr, ab_r, ab_i = zoh(are_b_ref[...], aim_b_ref[...], ldt_b_ref[...])
    li = aim_b_ref[...]
    nr, ni = ab_r - 1.0, ab_i
    den = lr * lr + li * li
    kr = (nr * lr + ni * li) / den
    ki = (ni * lr - nr * li) / den
    br, bi = bre_ref[...], bim_ref[...]
    bbr_ref[...] = kr * br - ki * bi
    bbi_ref[...] = kr * bi + ki * br


def _s5_prep(a_re, a_im, log_dt, b_re, b_im):
    lead = a_re.shape[:3]
    r = lead[0] * lead[1] * lead[2]
    p, h = S5_STATE, S5_GROUP_CH
    a2 = lambda t: t.reshape(r, p)
    ab = lambda t: jnp.broadcast_to(t.reshape(r, p, 1), (r, p, h)).reshape(r, p * h)
    ldt = jnp.broadcast_to(log_dt.reshape(r, 1), (r, p))
    ldt_b = jnp.broadcast_to(log_dt.reshape(r, 1), (r, p * h))
    small = jax.ShapeDtypeStruct((r, p), F32)
    big = jax.ShapeDtypeStruct((r, p * h), F32)
    abr, abi, bbr, bbi = pl.pallas_call(
        _s5_prep_body, out_shape=(small, small, big, big), name="s5_prep",
    )(a2(a_re), a2(a_im), ldt, ab(a_re), ab(a_im), ldt_b, b_re.reshape(r, p * h), b_im.reshape(r, p * h))
    return (abr.reshape(*lead, p), abi.reshape(*lead, p),
            bbr.reshape(*lead, p, h), bbi.reshape(*lead, p, h))


def _in_proj_body(*refs, rope, sub):
    if rope:
        (x_ref, sh_ref, sc_ref, g_ref, w_ref, gq_ref, gk_ref, hm_ref, cos_ref, sa_ref, sb_ref,
         qa_ref, ka_ref, va_ref, us_ref, nq_ref, nk_ref, nv_ref, fn_ref) = refs
    else:
        (x_ref, sh_ref, sc_ref, g_ref, w_ref, gq_ref, gk_ref, hm_ref,
         qa_ref, ka_ref, va_ref, us_ref, nq_ref, nk_ref, nv_ref, fn_ref) = refs
    hm = hm_ref[...]
    kw = OFF_ATT_V - OFF_ATT_K
    scale = HEAD_DIM ** -0.5 * LOG2E
    lane = lax.broadcasted_iota(jnp.int32, (sub, HEAD_DIM), 1)
    one_col = jnp.where(lane == 0, 1.0, 0.0)

    def head_norm(t, g, avg):
        ms = jnp.dot((t * t).astype(BF16), avg, preferred_element_type=F32)
        return t * lax.rsqrt(ms + EPS) * g

    for r0 in range(0, x_ref.shape[0], sub):
        rs = slice(r0, r0 + sub)
        h = _rms(x_ref[rs, :], g_ref[...]) * (1.0 + sc_ref[...]) + sh_ref[...]
        hb = h.astype(BF16)
        z_lo = jnp.dot(hb, w_ref[:, :OFF_NA_K], preferred_element_type=F32)
        z_hi = jnp.dot(hb, w_ref[:, OFF_NA_K:], preferred_element_type=F32)
        q = head_norm(z_lo[:, OFF_ATT_Q:OFF_ATT_K], gq_ref[...], hm)
        k = head_norm(z_lo[:, OFF_ATT_K:OFF_ATT_V], gk_ref[...], hm[:kw, :kw])
        if rope:
            def rot(t):
                w = t.shape[-1]
                half = HEAD_DIM // 4
                return (t * cos_ref[rs, :w] + pltpu.roll(t, w - half, 1) * sa_ref[rs, :w]
                        + pltpu.roll(t, half, 1) * sb_ref[rs, :w])
            q, k = rot(q), rot(k)
        qa_ref[rs, :] = (q * scale).astype(qa_ref.dtype)
        ka_ref[rs, :] = k.astype(ka_ref.dtype)
        v = z_lo[:, OFF_ATT_V:OFF_S5]
        v_ext = jnp.concatenate(
            sum(([v[:, HEAD_DIM * i:HEAD_DIM * (i + 1)], one_col] for i in range(ATT_KV_HEADS)), []), axis=-1)
        va_ref[:, rs] = v_ext.T.astype(va_ref.dtype)
        us_ref[rs, :] = z_lo[:, OFF_S5:OFF_NA_Q].astype(us_ref.dtype)
        nq_ref[rs, :] = (z_lo[:, OFF_NA_Q:OFF_NA_K] * scale).astype(nq_ref.dtype)
        nk_ref[rs, :] = z_hi[:, :GW].astype(nk_ref.dtype)
        nv_ref[rs, :] = z_hi[:, GW:2 * GW].astype(nv_ref.dtype)
        fn_ref[rs, :] = z_hi[:, 2 * GW:].astype(fn_ref.dtype)


def _in_proj(x, mod_t, layer, mod_row, g_pre, w_in, gq, gk, hm, rope_tabs, tm):
    b, n, d = x.shape
    rope = rope_tabs is not None
    row = lambda w: pl.BlockSpec((None, tm, w), lambda bi, i: (bi, i, 0))
    in_specs = [row(d), _mod_spec(mod_t, layer, 0, mod_row), _mod_spec(mod_t, layer, 1, mod_row)]
    in_specs += [_layer_spec(t, layer) for t in (g_pre, w_in, gq, gk)] + [_const_spec(hm.shape)]
    args = [x, mod_t, mod_t, g_pre, w_in, gq, gk, hm]
    if rope:
        in_specs += [pl.BlockSpec((tm, GW), lambda bi, i: (i, 0))] * 3
        args += list(rope_tabs)
    kvw = ATT_KV_HEADS * HEAD_DIM
    widths = [GW, kvw, None, GW, GW, GW, GW, GW]
    out_specs = [pl.BlockSpec((None, 2 * kvw, tm), lambda bi, i: (bi, 0, i)) if w is None else row(w) for w in widths]
    out_shape = [jax.ShapeDtypeStruct((b, 2 * kvw, n) if w is None else (b, n, w), BF16) for w in widths]
    return pl.pallas_call(
        functools.partial(_in_proj_body, rope=rope, sub=min(256, tm)),
        grid=(b, n // tm), in_specs=in_specs, out_specs=out_specs, out_shape=out_shape,
        compiler_params=_params("parallel", "parallel"), name="in_proj_rope" if rope else "in_proj",
    )(*args)


def _col_max(s):
    for rows in (256, 64):
        if s.shape[0] > rows and s.shape[0] % rows == 0:
            s = jnp.max(s.reshape(s.shape[0] // rows, rows, s.shape[1]), axis=0)
    return jnp.max(s, axis=0, keepdims=True)


def _gqa_body(*refs, n_heads, group, two_sets, sub):
    if two_sets:
        q_ref, k1_ref, v1t_ref, k2_ref, v2t_ref, o_ref = refs
    else:
        q_ref, k1_ref, v1t_ref, o_ref = refs
    key_sets = [(k1_ref, v1t_ref)] + ([(k2_ref, v2t_ref)] if two_sets else [])

    def scores(unit):
        r0, h = unit
        ks = slice(HEAD_DIM * (h // group), HEAD_DIM * (h // group + 1))
        qh = q_ref[r0:r0 + sub, HEAD_DIM * h:HEAD_DIM * (h + 1)]
        s = [_nt_dot(k_ref[:, ks], qh).astype(BF16) for k_ref, _ in key_sets]
        m = functools.reduce(jnp.maximum, [_col_max(t) for t in s])
        return s, m

    def weighted_values(h, s, m):
        vr = slice(2 * HEAD_DIM * (h // group), 2 * HEAD_DIM * (h // group + 1))
        o = None
        for t, (_, vt_ref) in zip(s, key_sets):
            for c0 in range(0, t.shape[0], KEY_CHUNK):
                p = jnp.exp2(t[c0:c0 + KEY_CHUNK] - m)
                part = jnp.dot(vt_ref[vr, c0:c0 + KEY_CHUNK], p, preferred_element_type=F32)
                o = part if o is None else o + part
        return o[:HEAD_DIM] / o[HEAD_DIM:HEAD_DIM + 1]

    units = [(r0, h) for r0 in range(0, q_ref.shape[0], sub) for h in range(n_heads)]
    outs = []
    nxt = scores(units[0])
    for i, (r0, h) in enumerate(units):
        cur, nxt = nxt, (scores(units[i + 1]) if i + 1 < len(units) else None)
        outs.append(weighted_values(h, *cur))
        if h == n_heads - 1:
            o_ref[r0:r0 + sub, :] = jnp.concatenate(outs, axis=0).T.astype(o_ref.dtype)
            outs = []


def _gqa(q, k1, v1t, k2=None, v2t=None, *, group, tq):
    b, nq, qw = q.shape
    two_sets = k2 is not None
    full = lambda t: pl.BlockSpec((None,) + t.shape[1:], lambda bi, i: (bi, 0, 0))
    args = [q, k1, v1t] + ([k2, v2t] if two_sets else [])
    in_specs = [pl.BlockSpec((None, tq, qw), lambda bi, i: (bi, i, 0))] + [full(t) for t in args[1:]]
    return pl.pallas_call(
        functools.partial(_gqa_body, n_heads=qw // HEAD_DIM, group=group, two_sets=two_sets, sub=min(256, tq)),
        grid=(b, nq // tq), in_specs=in_specs,
        out_specs=pl.BlockSpec((None, tq, qw), lambda bi, i: (bi, i, 0)),
        out_shape=jax.ShapeDtypeStruct((b, nq, qw), BF16),
        compiler_params=_params("parallel", "parallel"), name="gqa2" if two_sets else "gqa1",
    )(*args)


def _mha_body(q_ref, k_ref, v_ref, o_ref, *, n_heads):
    for h in range(n_heads):
        hs = slice(HEAD_DIM * h, HEAD_DIM * (h + 1))
        s = _nt_dot(q_ref[:, hs], k_ref[:, hs])
        p = jnp.exp2(s - jnp.max(s, axis=-1, keepdims=True))
        o = jnp.dot(p.astype(BF16), v_ref[:, hs], preferred_element_type=F32)
        o_ref[:, hs] = (o / jnp.sum(p, axis=-1, keepdims=True)).astype(o_ref.dtype)


def _mha(q, k, v, *, tq):
    b, nq, qw = q.shape
    full = lambda t: pl.BlockSpec((None,) + t.shape[1:], lambda bi, i: (bi, 0, 0))
    return pl.pallas_call(
        functools.partial(_mha_body, n_heads=qw // HEAD_DIM),
        grid=(b, nq // tq),
        in_specs=[pl.BlockSpec((None, tq, qw), lambda bi, i: (bi, i, 0)), full(k), full(v)],
        out_specs=pl.BlockSpec((None, tq, qw), lambda bi, i: (bi, i, 0)),
        out_shape=jax.ShapeDtypeStruct((b, nq, qw), BF16),
        compiler_params=_params("parallel", "parallel"), name="mha",
    )(q, k, v)


def _na_body(q_ref, k_ref, v_ref, kc_ref, vc_ref, bias_ref, o_ref, *, rt, k_r, rows):
    i = pl.program_id(1)
    kc = kc_ref[...]
    vc = vc_ref[...]
    lane_head = lax.broadcasted_iota(jnp.int32, (GRID_W, GW), 1) // HEAD_DIM
    def scores(j):
        r = i * rt + j
        rs = jnp.clip(r - k_r // 2, 0, rows - k_r)
        start = pl.multiple_of(rs * GRID_W, GRID_W)
        q = q_ref[j * GRID_W:(j + 1) * GRID_W, :]
        q4 = jnp.concatenate([jnp.where(lane_head == h, q, jnp.zeros_like(q)) for h in range(NA_HEADS)], axis=0)
        s_loc = (_nt_dot(q4, k_ref[pl.ds(start, k_r * GRID_W), :]) + bias_ref[r - rs]).astype(BF16)
        s_ctx = _nt_dot(q4, kc).astype(BF16)
        return start, s_loc, s_ctx

    def weighted_values(j, start, s_loc, s_ctx):
        m = jnp.maximum(jnp.max(s_loc, axis=-1, keepdims=True), jnp.max(s_ctx, axis=-1, keepdims=True))
        p_loc = jnp.exp2(s_loc - m)
        p_ctx = jnp.exp2(s_ctx - m)
        l = (jnp.sum(p_loc.astype(F32), axis=-1, keepdims=True)
             + jnp.sum(p_ctx.astype(F32), axis=-1, keepdims=True))
        o4 = (jnp.dot(p_loc, v_ref[pl.ds(start, k_r * GRID_W), :], preferred_element_type=F32)
              + jnp.dot(p_ctx, vc, preferred_element_type=F32)) / l
        o = jnp.zeros((GRID_W, GW), F32)
        for h in range(NA_HEADS):
            o = o + jnp.where(lane_head == h, o4[h * GRID_W:(h + 1) * GRID_W, :], 0.0)
        o_ref[j * GRID_W:(j + 1) * GRID_W, :] = o.astype(o_ref.dtype)

    nxt = scores(0)
    for j in range(rt):
        cur, nxt = nxt, (scores(j + 1) if j + 1 < rt else None)
        weighted_values(j, *cur)


def _na(q, k, v, kc, vc, bias, layer, rt):
    b, n, _ = q.shape
    rows = n // GRID_W
    k_r = bias.shape[1]
    full = lambda t: pl.BlockSpec((None,) + t.shape[1:], lambda bi, i: (bi, 0, 0))
    return pl.pallas_call(
        functools.partial(_na_body, rt=rt, k_r=k_r, rows=rows),
        grid=(b, rows // rt),
        in_specs=[pl.BlockSpec((None, rt * GRID_W, GW), lambda bi, i: (bi, i, 0)),
                  full(k), full(v), full(kc), full(vc), _layer_spec(bias, layer)],
        out_specs=pl.BlockSpec((None, rt * GRID_W, GW), lambda bi, i: (bi, i, 0)),
        out_shape=jax.ShapeDtypeStruct((b, n, GW), BF16),
        compiler_params=_params("parallel", "parallel"), name="na",
    )(q, k, v, kc, vc, bias)


def _na_bias_table(rel_bias, rows):
    k_r = min(NA_ROWS, rows)
    cols = np.arange(GRID_W)
    col_start = np.clip(cols - NA_COLS // 2, 0, GRID_W - NA_COLS)
    inside = (cols[None, :] >= col_start[:, None]) & (cols[None, :] < col_start[:, None] + NA_COLS)
    rel_c = cols[None, :] - cols[:, None] + (NA_COLS - 1)
    rel_r = np.arange(k_r)[None, :] - np.arange(k_r)[:, None] + (NA_ROWS - 1)
    pick_r = (rel_r[:, :, None] == np.arange(2 * NA_ROWS - 1)).astype(np.float32)
    pick_c = ((rel_c[:, :, None] == np.arange(2 * NA_COLS - 1)) & inside[:, :, None]).astype(np.float32)
    t = jnp.einsum("cai,lhim->lcham", pick_r, rel_bias.astype(F32), precision=HIGHEST)
    t = jnp.einsum("lcham,jkm->lchjak", t, pick_c, precision=HIGHEST)
    t = t + np.where(inside, 0.0, MASK_VALUE).astype(np.float32)[None, None, None, :, None, :]
    return (t * LOG2E).reshape(rel_bias.shape[0], k_r, NA_HEADS * GRID_W, k_r * GRID_W)


def _s5_scan(hs_ref, st_ref, a_ref, *, tc, nb, n_state, col_w, reverse):
    for c0 in range(0, n_state, col_w):
        re = slice(c0, c0 + col_w)
        im = slice(n_state + c0, n_state + c0 + col_w)
        ar = jnp.broadcast_to(a_ref[:, re], (nb, col_w))
        ai = jnp.broadcast_to(a_ref[:, im], (nb, col_w))

        hr, hi = st_ref[:, re], st_ref[:, im]
        for t in range(tc):
            rows = slice((tc - 1 - t if reverse else t) * nb, (tc - t if reverse else t + 1) * nb)
            hr, hi = ar * hr - ai * hi + hs_ref[rows, re], ar * hi + ai * hr + hs_ref[rows, im]
            hs_ref[rows, re] = hr
            hs_ref[rows, im] = hi
        st_ref[:, re] = hr
        st_ref[:, im] = hi


def _s5_fwd_body(uc_ref, ul_ref, perm_ref, bd_ref, cd_ref, a_ref, ut_ref, yf_ref, hs0_ref, hs1_ref, st_ref,
                 *, n_c, scan):
    j = pl.program_id(0)

    @pl.when(j == 0)
    def _():
        st_ref[...] = jnp.zeros_like(st_ref)
        hs1_ref[...] = jnp.zeros_like(hs1_ref)

    def step(cur, prev):
        nb, tc, w = uc_ref.shape
        ts = perm_ref.shape[0] // nb
        u_bm = jnp.where(j < n_c, uc_ref[...], ul_ref[...])
        ut = jnp.concatenate(
            [jnp.dot(perm_ref[...], u_bm[:, s:s + ts, :].reshape(nb * ts, w), preferred_element_type=F32)
             for s in range(0, tc, ts)], axis=0).astype(ut_ref.dtype)
        ut_ref[...] = ut
        cur[...] = _dot2(ut, bd_ref[...])
        scan(prev, st_ref, a_ref, reverse=False)
        yf_ref[...] = _dot2(prev[...].astype(BF16), cd_ref[...])

    pl.when(j % 2 == 0)(lambda: step(hs0_ref, hs1_ref))
    pl.when(j % 2 == 1)(lambda: step(hs1_ref, hs0_ref))


def _s5_bwd_body(ut_ref, utp_ref, yf_ref, permt_ref, bd_ref, cd_ref, a_ref, dsk_ref, wglu_ref, oc_ref, ol_ref,
                 hs0_ref, hs1_ref, st_ref, *, n_c, scan):
    j = pl.program_id(0)

    @pl.when(j == 0)
    def _():
        st_ref[...] = jnp.zeros_like(st_ref)
        hs1_ref[...] = jnp.zeros_like(hs1_ref)

    def step(cur, prev):
        cur[...] = _dot2(ut_ref[...], bd_ref[...])
        scan(prev, st_ref, a_ref, reverse=True)
        t = (dsk_ref[...] * utp_ref[...].astype(F32) + yf_ref[...]
             + _dot2(prev[...].astype(BF16), cd_ref[...]))
        g = jax.nn.gelu(t)
        ob = (g * jax.nn.sigmoid(_dot2(g.astype(BF16), wglu_ref[...]))).astype(BF16)
        nb, tc, w = oc_ref.shape
        rows = permt_ref.shape[0]
        groups = [jnp.dot(permt_ref[...], ob[r0:r0 + rows], preferred_element_type=F32).astype(BF16)
                  .reshape(nb, rows // nb, w) for r0 in range(0, tc * nb, rows)]

        def write(o_ref):
            for s, grp in enumerate(groups):
                o_ref[:, s * grp.shape[1]:(s + 1) * grp.shape[1], :] = grp

        pl.when(j <= n_c)(lambda: write(oc_ref))
        pl.when(j > n_c)(lambda: write(ol_ref))

    pl.when(j % 2 == 0)(lambda: step(hs0_ref, hs1_ref))
    pl.when(j % 2 == 1)(lambda: step(hs1_ref, hs0_ref))


def _s5(usc, us, bd, cd, a_row, dsk, wglu, layer, tc):
    nb, lc, _ = usc.shape
    n = us.shape[1]
    n_c, n_l = lc // tc, n // tc
    n_all = n_c + n_l
    n_state2 = bd.shape[-1]
    rows = tc * nb
    ts = math.gcd(tc, BF16_SUBLANES)
    r = np.arange(ts * nb)
    perm = np.zeros((ts * nb, ts * nb), np.float32)
    perm[r, (r % nb) * ts + r // nb] = 1.0
    scan = functools.partial(_s5_scan, tc=tc, nb=nb, n_state=n_state2 // 2, col_w=512)
    scratch = [pltpu.VMEM((rows, n_state2), F32), pltpu.VMEM((nb, n_state2), F32)]
    seg = lambda idx: pl.BlockSpec((nb, tc, GW), lambda j: (0, idx(j), 0))
    tm_rows = lambda idx: pl.BlockSpec((rows, GW), lambda j: (idx(j), 0))

    ut, yf = pl.pallas_call(
        functools.partial(_s5_fwd_body, n_c=n_c, scan=scan), grid=(n_all + 1,),
        in_specs=[seg(lambda j: jnp.minimum(j, n_c - 1)), seg(lambda j: jnp.clip(j - n_c, 0, n_l - 1)),
                  _const_spec(perm.shape)] + [_layer_spec(t, layer, 0) for t in (bd, cd, a_row)],
        out_specs=[tm_rows(lambda j: jnp.minimum(j, n_all - 1)), tm_rows(lambda j: jnp.maximum(j - 1, 0))],
        out_shape=[jax.ShapeDtypeStruct((n_all * rows, GW), BF16), jax.ShapeDtypeStruct((n_all * rows, GW), F32)],
        scratch_shapes=[scratch[0], scratch[0], scratch[1]],
        compiler_params=_params("arbitrary"), name="s5_fwd",
    )(usc, us, jnp.asarray(perm.astype(BF16)), bd, cd, a_row)

    def blk(j):
        j = jnp.clip(j, 0, n_all - 1)
        return jnp.where(j < n_c, n_c - 1 - j, n_c + n_all - 1 - j)

    prev = lambda j: jnp.maximum(j - 1, 0)
    return pl.pallas_call(
        functools.partial(_s5_bwd_body, n_c=n_c, scan=scan), grid=(n_all + 1,),
        in_specs=[tm_rows(blk), tm_rows(lambda j: blk(prev(j))), tm_rows(lambda j: blk(prev(j))),
                  _const_spec(perm.shape)] + [_layer_spec(t, layer, 1) for t in (bd, cd, a_row)]
        + [_layer_spec(dsk, layer), _layer_spec(wglu, layer)],
        out_specs=[seg(lambda j: jnp.maximum(n_c - 1 - prev(j), 0)),
                   seg(lambda j: n_l - 1 - jnp.maximum(prev(j) - n_c, 0))],
        out_shape=[jax.ShapeDtypeStruct(usc.shape, BF16), jax.ShapeDtypeStruct(us.shape, BF16)],
        scratch_shapes=[scratch[0], scratch[0], scratch[1]], compiler_params=_params("arbitrary"), name="s5_bwd",
    )(ut, ut, yf, jnp.asarray(perm.T.astype(BF16)), bd, cd, a_row, dsk, wglu)


def _s5_matrices(abr, abi, bbr, bbi, c_re, c_im):
    g, p, h = S5_GROUPS, S5_STATE, S5_GROUP_CH
    lead = abr.shape[:2]
    eye = np.eye(g, dtype=np.float32)[None, None, :, None, :, None]
    spread = lambda t: (jnp.swapaxes(t, -1, -2)[:, :, :, :, None, :] * eye).reshape(
        *lead, g * t.shape[-1], g * t.shape[-2])
    bd = jnp.concatenate([spread(bbr), spread(bbi)], axis=-1)
    cd = jnp.concatenate([spread(c_re.astype(F32)), spread(-c_im.astype(F32))], axis=-2)
    a_row = jnp.concatenate([abr.reshape(*lead, 1, g * p), abi.reshape(*lead, 1, g * p)], axis=-1)
    return bd.astype(BF16), cd.astype(BF16), a_row


def _fnet_body(u_ref, mirror_ref, dft_ref, ccs_ref, w_ref, b_ref, o_ref, ue_ref):
    h = u_ref.shape[0] // 2

    @pl.when(pl.program_id(1) == 0)
    def _():
        uf = _dot2(mirror_ref[...], u_ref[h:2 * h, :]).astype(BF16)
        t = jnp.dot(u_ref[0:h, :], ccs_ref[...], preferred_element_type=F32)
        tf = jnp.dot(uf, ccs_ref[...], preferred_element_type=F32)
        row0 = lax.broadcasted_iota(jnp.int32, (h, GW), 0) == 0
        ue_ref[0:h, :] = (t[:, :GW] + jnp.where(row0, 0.0, tf[:, :GW])).astype(BF16)
        ue_ref[h:2 * h, :] = jnp.where(row0, tf[:, :GW], t[:, GW:] - tf[:, GW:]).astype(BF16)

    tk = o_ref.shape[0]
    rows = pl.ds(pl.multiple_of(pl.program_id(1) * tk, tk), tk)
    f = _dot2(dft_ref[rows, :], ue_ref[...])
    o = jnp.dot(f.astype(BF16), w_ref[...], preferred_element_type=F32) + b_ref[...]
    o_ref[...] = o.astype(o_ref.dtype)


def _fnet(u, dft, ccs, w, bias, layer, tk):
    b, n, _ = u.shape
    h = n // 2
    mirror = np.zeros((h, h), np.float32)
    mirror[np.arange(h), (h - np.arange(h)) % h] = 1.0
    return pl.pallas_call(
        _fnet_body, grid=(b, n // tk),
        in_specs=[pl.BlockSpec((None, n, GW), lambda bi, k: (bi, 0, 0)), _const_spec(mirror.shape),
                  _const_spec(dft.shape),
                  _const_spec(ccs.shape), _layer_spec(w, layer), _layer_spec(bias, layer)],
        out_specs=pl.BlockSpec((None, tk, GW), lambda bi, k: (bi, k, 0)),
        out_shape=jax.ShapeDtypeStruct((b, n, GW), BF16),
        scratch_shapes=[pltpu.VMEM((n, GW), BF16)],
        compiler_params=_params("parallel", "arbitrary"), name="fnet",
    )(u, jnp.asarray(mirror.astype(BF16)), dft, ccs, w, bias)


def _dft_matrices(n):
    ch = GW // FN_GROUPS
    h = n // 2
    k = np.arange(n)
    ang = (2.0 * np.pi / n) * ((k[:, None] * np.arange(h)[None, :]) % n)
    m = -np.sin(ang)
    m[:, 0] = 1.0 - 2.0 * (k % 2)
    dft = np.concatenate([np.cos(ang), m], axis=1)
    c = np.arange(GW)
    same = (c[:, None] // ch) == (c[None, :] // ch)
    angc = (2.0 * np.pi / ch) * (((c[:, None] % ch) * (c[None, :] % ch)) % ch)
    norm = 1.0 / math.sqrt(n * ch)
    ccs = np.concatenate([np.where(same, np.cos(angc), 0.0), np.where(same, np.sin(angc), 0.0)], axis=1) * norm
    return (jnp.asarray(dft.astype(np.float32)).astype(BF16),
            jnp.asarray(ccs.astype(np.float32)).astype(BF16))


def _out_mlp_body(x_ref, oa_ref, ob_ref, on_ref, od_ref, g1_ref, sh2_ref, sc2_ref, g2_ref,
                  wout_ref, gpm_ref, gpre_ref, gpost_ref, w1_ref, w2_ref, o_ref, *, ff_chunk, sub):
    d_ff = w1_ref.shape[1]
    for r0 in range(0, x_ref.shape[0], sub):
        rs = slice(r0, r0 + sub)
        cat = jnp.concatenate([oa_ref[rs, :], ob_ref[rs, :], on_ref[rs, :], od_ref[rs, :]], axis=-1)
        y = jnp.dot(cat, wout_ref[...], preferred_element_type=F32)
        x1 = x_ref[rs, :] + g1_ref[...] * _rms(y, gpm_ref[...])
        h = (_rms(x1, gpre_ref[...]) * (1.0 + sc2_ref[...]) + sh2_ref[...]).astype(BF16)
        m = jnp.zeros(x1.shape, F32)
        for c0 in range(0, d_ff, ff_chunk):
            a = jnp.maximum(jnp.dot(h, w1_ref[:, c0:c0 + ff_chunk], preferred_element_type=F32), 0.0)
            m = m + jnp.dot((a * a).astype(BF16), w2_ref[c0:c0 + ff_chunk, :], preferred_element_type=F32)
        o_ref[rs, :] = x1 + g2_ref[...] * _rms(m, gpost_ref[...])


def _out_mlp(x, oa, ob, on, od, mod_t, layer, mod_row, wout, gpm, gpre, gpost, w1, w2, tm):
    b, n, d = x.shape
    row = lambda w: pl.BlockSpec((None, tm, w), lambda bi, i: (bi, i, 0))
    consts = [wout, gpm, gpre, gpost, w1, w2]
    return pl.pallas_call(
        functools.partial(_out_mlp_body, ff_chunk=1024, sub=min(512, tm)),
        grid=(b, n // tm),
        in_specs=[row(d), row(GW), row(GW), row(GW), row(GW)]
        + [_mod_spec(mod_t, layer, which, mod_row) for which in (2, 3, 4, 5)]
        + [_layer_spec(t, layer) for t in consts],
        out_specs=row(d), out_shape=jax.ShapeDtypeStruct((b, n, d), F32),
        compiler_params=_params("parallel", "parallel"), name="out_mlp",
    )(x, oa, ob, on, od, mod_t, mod_t, mod_t, mod_t, *consts)


def _rope_tables(n):
    half = HEAD_DIM // 4
    pos = np.arange(n)
    jj = np.arange(GW) % HEAD_DIM
    inv = ROPE_BASE ** (-(jj % half) / half)
    p = np.where((jj // (2 * half) == 0)[None, :], (pos // GRID_W)[:, None], (pos % GRID_W)[:, None])
    ang = p * inv[None, :]
    first = ((jj % (2 * half)) < half)[None, :]
    sin = np.sin(ang)
    tabs = (np.cos(ang), np.where(first, -sin, 0.0), np.where(first, 0.0, sin))
    return tuple(jnp.asarray(t.astype(np.float32)) for t in tabs)


def kernel(x, c, ctx, c_ctx, w_ada, b_ada, g_pre_mix, g_post_mix, g_pre_mlp, g_post_mlp, w_in, g_q_attn, g_k_attn,
           s5_a_re, s5_a_im, s5_log_dt, s5_b_re, s5_b_im, s5_c_re, s5_c_im, s5_d, w_s5_glu, na_rel_bias,
           w_fnet, b_fnet, w_out, w_mlp1, w_mlp2):
    b, n, d = x.shape
    lc = ctx.shape[1]
    depth = w_ada.shape[0]
    rows = n // GRID_W

    n_rows = -(-(b + 1) // 8) * 8
    cc = jnp.concatenate([c, c_ctx[None, :], jnp.zeros((n_rows - b - 1, d), c.dtype)], axis=0)
    mod_t = _ada(cc, w_ada, b_ada)

    vec = lambda t: t.reshape(depth, 1, -1).astype(F32)
    abr, abi, bbr, bbi = _s5_prep(s5_a_re, s5_a_im, s5_log_dt, s5_b_re, s5_b_im)
    bd, cd, a_row = _s5_matrices(abr, abi, bbr, bbi, s5_c_re, s5_c_im)
    na_bias = _na_bias_table(na_rel_bias, rows)
    w_in_b, w_glu_b, w_fn_b = w_in.astype(BF16), w_s5_glu.astype(BF16), w_fnet.astype(BF16)
    mlp_consts = (w_out.astype(BF16), vec(g_post_mix), vec(g_pre_mlp), vec(g_post_mlp),
                  w_mlp1.astype(BF16), w_mlp2.astype(BF16))
    g_pre, d_skip, b_fn = vec(g_pre_mix), vec(s5_d), vec(b_fnet)
    gq = jnp.tile(vec(g_q_attn), (1, 1, ATT_HEADS))
    gk = jnp.tile(vec(g_k_attn), (1, 1, ATT_KV_HEADS))
    rope_tabs = _rope_tables(n)
    lane = np.arange(GW)
    hm = jnp.asarray(np.where((lane[:, None] // HEAD_DIM) == (lane[None, :] // HEAD_DIM), 1.0 / HEAD_DIM, 0.0)
                     .astype(np.float32).astype(BF16))
    dft_n, ccs_n = _dft_matrices(n)
    dft_c, ccs_c = _dft_matrices(lc)

    tm_lat = min(1024, n)
    tm_c = min(256, lc)
    tc = math.gcd(64, math.gcd(n, lc))
    group = ATT_HEADS // ATT_KV_HEADS

    xc = ctx
    for l in range(depth):
        need_ctx = l < depth - 1
        qa, ka, va, us, nq, nk, nv, fn = _in_proj(x, mod_t, l, None, g_pre, w_in_b, gq, gk, hm, rope_tabs, tm_lat)
        qac, kac, vac, usc, nqc, nkc, nvc, fnc = _in_proj(xc, mod_t, l, b, g_pre, w_in_b, gq, gk, hm, None, tm_c)

        oa = _gqa(qa, ka, va, kac, vac, group=group, tq=min(1024, n))
        on = _na(nq, nk, nv, nkc, nvc, na_bias, l, rt=min(16, rows))
        od = _fnet(fn, dft_n, ccs_n, w_fn_b, b_fn, l, tk=min(512, n))
        obc, ob = _s5(usc, us, bd, cd, a_row, d_skip, w_glu_b, l, tc)
        x_new = _out_mlp(x, oa, ob, on, od, mod_t, l, None, *mlp_consts, tm_lat)
        if need_ctx:
            oac = _gqa(qac, kac, vac, group=group, tq=tm_c)
            onc = _mha(nqc, nkc, nvc, tq=tm_c)
            odc = _fnet(fnc, dft_c, ccs_c, w_fn_b, b_fn, l, tk=tm_c)
            xc = _out_mlp(xc, oac, obc, onc, odc, mod_t, l, b, *mlp_consts, tm_c)
        x = x_new
    return x
```

```python
import functools
import math

import jax
import jax.numpy as jnp
import numpy as np
from jax import lax
from jax.experimental import pallas as pl
from jax.experimental.pallas import tpu as pltpu

F32 = jnp.float32
BF16 = jnp.bfloat16
HIGHEST = lax.Precision.HIGHEST

HEAD_DIM = 64
GRID_W = 64
N_MOD = 6
EPS = 1e-6
ATT_HEADS = 4
ATT_KV_HEADS = 2
ROPE_BASE = 10000.0
S5_GROUPS = 16
S5_GROUP_CH = 16
S5_STATE = 64
S5_MIN_DECAY = 1e-4
NA_HEADS = 4
NA_ROWS = 8
NA_COLS = 16
FN_GROUPS = 4
GW = 256
MASK_VALUE = -1e30
LOG2E = 1.4426950408889634
KEY_CHUNK = 1024
BF16_SUBLANES = 16

OFF_ATT_Q, OFF_ATT_K, OFF_ATT_V, OFF_S5 = 0, 256, 384, 512
OFF_NA_Q, OFF_NA_K, OFF_NA_V, OFF_FN, IN_WIDTH = 768, 1024, 1280, 1536, 1792

VMEM_LIMIT = 56 * 1024 * 1024


def _params(*sem):
    return pltpu.CompilerParams(dimension_semantics=sem, vmem_limit_bytes=VMEM_LIMIT)


def _const_spec(shape):
    return pl.BlockSpec(shape, lambda *_: (0,) * len(shape), pipeline_mode=pl.Buffered(1))


def _layer_spec(arr, *lead):
    k = len(lead)
    return pl.BlockSpec((None,) * k + arr.shape[k:], lambda *_: tuple(lead) + (0,) * (arr.ndim - k),
                        pipeline_mode=pl.Buffered(1))


def _rms(x, g):
    return x * lax.rsqrt(jnp.mean(x * x, axis=-1, keepdims=True) + EPS) * g


def _dot2(a, b):
    half = a.shape[0] // 2
    return jnp.concatenate([jnp.dot(a[:half], b, preferred_element_type=F32),
                            jnp.dot(a[half:], b, preferred_element_type=F32)], axis=0)


def _nt_dot(a, b):
    return lax.dot_general(a, b, (((1,), (1,)), ((), ())), preferred_element_type=F32)


def _ada_body(c_ref, w_ref, b_ref, o_ref):
    c = c_ref[...]
    s = c * jax.nn.sigmoid(c)
    o_ref[...] = jnp.dot(s, w_ref[...], precision=HIGHEST, preferred_element_type=F32) + b_ref[...]


def _ada(cc, w_ada, b_ada):
    depth, d, n_out = w_ada.shape
    rows = cc.shape[0]
    out = pl.pallas_call(
        _ada_body,
        grid=(depth, n_out // d),
        in_specs=[
            pl.BlockSpec((rows, d), lambda l, j: (0, 0)),
            pl.BlockSpec((None, d, d), lambda l, j: (l, 0, j)),
            pl.BlockSpec((None, 1, d), lambda l, j: (l, 0, j)),
        ],
        out_specs=pl.BlockSpec((None, None, rows, d), lambda l, j: (l, j, 0, 0)),
        out_shape=jax.ShapeDtypeStruct((depth, n_out // d, rows, d), F32),
        compiler_params=_params("parallel", "parallel"),
        name="ada",
    )(cc, w_ada, b_ada.reshape(depth, 1, n_out))
    return out.reshape(depth, n_out // d, rows, 1, d)


def _mod_spec(mod_t, layer, which, row):
    d = mod_t.shape[-1]
    return pl.BlockSpec((None, None, None, 1, d),
                        lambda bi, i: (layer, which, bi if row is None else row, 0, 0))


def _s5_prep_body(are_ref, aim_ref, ldt_ref, are_b_ref, aim_b_ref, ldt_b_ref, bre_ref, bim_ref,
                  abr_ref, abi_ref, bbr_ref, bbi_ref):
    def zoh(a_re, a_im, log_dt):
        lr = jnp.minimum(a_re, -S5_MIN_DECAY)
        dt = jnp.exp(log_dt)
        mag = jnp.exp(lr * dt)
        ab_r = mag * jnp.cos(a_im * dt)
        ab_i = mag * jnp.sin(a_im * dt)
        return lr, ab_r, ab_i

    _, ab_r, ab_i = zoh(are_ref[...], aim_ref[...], ldt_ref[...])
    abr_ref[...] = ab_r
    abi_ref[...] = ab_i
    lr, ab_r, ab_i = zoh(are_b_ref[...], aim_b_ref[...], ldt_b_ref[...])
    li = aim_b_ref[...]
    nr, ni = ab_r - 1.0, ab_i
    den = lr * lr + li * li
    kr = (nr * lr + ni * li) / den
    ki = (ni * lr - nr * li) / den
    br, bi = bre_ref[...], bim_ref[...]
    bbr_ref[...] = kr * br - ki * bi
    bbi_ref[...] = kr * bi + ki * br


def _s5_prep(a_re, a_im, log_dt, b_re, b_im):
    lead = a_re.shape[:3]
    r = lead[0] * lead[1] * lead[2]
    p, h = S5_STATE, S5_GROUP_CH
    a2 = lambda t: t.reshape(r, p)
    ab = lambda t: jnp.broadcast_to(t.reshape(r, p, 1), (r, p, h)).reshape(r, p * h)
    ldt = jnp.broadcast_to(log_dt.reshape(r, 1), (r, p))
    ldt_b = jnp.broadcast_to(log_dt.reshape(r, 1), (r, p * h))
    small = jax.ShapeDtypeStruct((r, p), F32)
    big = jax.ShapeDtypeStruct((r, p * h), F32)
    abr, abi, bbr, bbi = pl.pallas_call(
        _s5_prep_body, out_shape=(small, small, big, big), name="s5_prep",
    )(a2(a_re), a2(a_im), ldt, ab(a_re), ab(a_im), ldt_b, b_re.reshape(r, p * h), b_im.reshape(r, p * h))
    return (abr.reshape(*lead, p), abi.reshape(*lead, p),
            bbr.reshape(*lead, p, h), bbi.reshape(*lead, p, h))


def _in_proj_body(*refs, rope, sub):
    if rope:
        (x_ref, sh_ref, sc_ref, g_ref, w_ref, gq_ref, gk_ref, hm_ref, cos_ref, sa_ref, sb_ref,
         qa_ref, ka_ref, va_ref, us_ref, nq_ref, nk_ref, nv_ref, fn_ref) = refs
    else:
        (x_ref, sh_ref, sc_ref, g_ref, w_ref, gq_ref, gk_ref, hm_ref,
         qa_ref, ka_ref, va_ref, us_ref, nq_ref, nk_ref, nv_ref, fn_ref) = refs
    hm = hm_ref[...]
    kw = OFF_ATT_V - OFF_ATT_K
    scale = HEAD_DIM ** -0.5 * LOG2E
    lane = lax.broadcasted_iota(jnp.int32, (sub, HEAD_DIM), 1)
    one_col = jnp.where(lane == 0, 1.0, 0.0)

    def head_norm(t, g, avg):
        ms = jnp.dot((t * t).astype(BF16), avg, preferred_element_type=F32)
        return t * lax.rsqrt(ms + EPS) * g

    for r0 in range(0, x_ref.shape[0], sub):
        rs = slice(r0, r0 + sub)
        h = _rms(x_ref[rs, :], g_ref[...]) * (1.0 + sc_ref[...]) + sh_ref[...]
        hb = h.astype(BF16)
        z_lo = jnp.dot(hb, w_ref[:, :OFF_NA_K], preferred_element_type=F32)
        z_hi = jnp.dot(hb, w_ref[:, OFF_NA_K:], preferred_element_type=F32)
        q = head_norm(z_lo[:, OFF_ATT_Q:OFF_ATT_K], gq_ref[...], hm)
        k = head_norm(z_lo[:, OFF_ATT_K:OFF_ATT_V], gk_ref[...], hm[:kw, :kw])
        if rope:
            def rot(t):
                w = t.shape[-1]
                half = HEAD_DIM // 4
                return (t * cos_ref[rs, :w] + pltpu.roll(t, w - half, 1) * sa_ref[rs, :w]
                        + pltpu.roll(t, half, 1) * sb_ref[rs, :w])
            q, k = rot(q), rot(k)
        qa_ref[rs, :] = (q * scale).astype(qa_ref.dtype)
        ka_ref[rs, :] = k.astype(ka_ref.dtype)
        v = z_lo[:, OFF_ATT_V:OFF_S5]
        v_ext = jnp.concatenate(
            sum(([v[:, HEAD_DIM * i:HEAD_DIM * (i + 1)], one_col] for i in range(ATT_KV_HEADS)), []), axis=-1)
        va_ref[:, rs] = v_ext.T.astype(va_ref.dtype)
        us_ref[rs, :] = z_lo[:, OFF_S5:OFF_NA_Q].astype(us_ref.dtype)
        nq_ref[rs, :] = (z_lo[:, OFF_NA_Q:OFF_NA_K] * scale).astype(nq_ref.dtype)
        nk_ref[rs, :] = z_hi[:, :GW].astype(nk_ref.dtype)
        nv_ref[rs, :] = z_hi[:, GW:2 * GW].astype(nv_ref.dtype)
        fn_ref[rs, :] = z_hi[:, 2 * GW:].astype(fn_ref.dtype)


def _in_proj(x, mod_t, layer, mod_row, g_pre, w_in, gq, gk, hm, rope_tabs, tm):
    b, n, d = x.shape
    rope = rope_tabs is not None
    row = lambda w: pl.BlockSpec((None, tm, w), lambda bi, i: (bi, i, 0))
    in_specs = [row(d), _mod_spec(mod_t, layer, 0, mod_row), _mod_spec(mod_t, layer, 1, mod_row)]
    in_specs += [_layer_spec(t, layer) for t in (g_pre, w_in, gq, gk)] + [_const_spec(hm.shape)]
    args = [x, mod_t, mod_t, g_pre, w_in, gq, gk, hm]
    if rope:
        in_specs += [pl.BlockSpec((tm, GW), lambda bi, i: (i, 0))] * 3
        args += list(rope_tabs)
    kvw = ATT_KV_HEADS * HEAD_DIM
    widths = [GW, kvw, None, GW, GW, GW, GW, GW]
    out_specs = [pl.BlockSpec((None, 2 * kvw, tm), lambda bi, i: (bi, 0, i)) if w is None else row(w) for w in widths]
    out_shape = [jax.ShapeDtypeStruct((b, 2 * kvw, n) if w is None else (b, n, w), BF16) for w in widths]
    return pl.pallas_call(
        functools.partial(_in_proj_body, rope=rope, sub=min(256, tm)),
        grid=(b, n // tm), in_specs=in_specs, out_specs=out_specs, out_shape=out_shape,
        compiler_params=_params("parallel", "parallel"), name="in_proj_rope" if rope else "in_proj",
    )(*args)


def _col_max(s):
    for rows in (256, 64):
        if s.shape[0] > rows and s.shape[0] % rows == 0:
            s = jnp.max(s.reshape(s.shape[0] // rows, rows, s.shape[1]), axis=0)
    return jnp.max(s, axis=0, keepdims=True)


def _gqa_body(*refs, n_heads, group, two_sets, sub):
    if two_sets:
        q_ref, k1_ref, v1t_ref, k2_ref, v2t_ref, o_ref = refs
    else:
        q_ref, k1_ref, v1t_ref, o_ref = refs
    key_sets = [(k1_ref, v1t_ref)] + ([(k2_ref, v2t_ref)] if two_sets else [])

    def scores(unit):
        r0, h = unit
        ks = slice(HEAD_DIM * (h // group), HEAD_DIM * (h // group + 1))
        qh = q_ref[r0:r0 + sub, HEAD_DIM * h:HEAD_DIM * (h + 1)]
        s = [_nt_dot(k_ref[:, ks], qh).astype(BF16) for k_ref, _ in key_sets]
        m = functools.reduce(jnp.maximum, [_col_max(t) for t in s])
        return s, m

    def weighted_values(h, s, m):
        vr = slice(2 * HEAD_DIM * (h // group), 2 * HEAD_DIM * (h // group + 1))
        o = None
        for t, (_, vt_ref) in zip(s, key_sets):
            for c0 in range(0, t.shape[0], KEY_CHUNK):
                p = jnp.exp2(t[c0:c0 + KEY_CHUNK] - m)
                part = jnp.dot(vt_ref[vr, c0:c0 + KEY_CHUNK], p, preferred_element_type=F32)
                o = part if o is None else o + part
        return o[:HEAD_DIM] / o[HEAD_DIM:HEAD_DIM + 1]

    units = [(r0, h) for r0 in range(0, q_ref.shape[0], sub) for h in range(n_heads)]
    outs = []
    nxt = scores(units[0])
    for i, (r0, h) in enumerate(units):
        cur, nxt = nxt, (scores(units[i + 1]) if i + 1 < len(units) else None)
        outs.append(weighted_values(h, *cur))
        if h == n_heads - 1:
            o_ref[r0:r0 + sub, :] = jnp.concatenate(outs, axis=0).T.astype(o_ref.dtype)
            outs = []


def _gqa(q, k1, v1t, k2=None, v2t=None, *, group, tq):
    b, nq, qw = q.shape
    two_sets = k2 is not None
    full = lambda t: pl.BlockSpec((None,) + t.shape[1:], lambda bi, i: (bi, 0, 0))
    args = [q, k1, v1t] + ([k2, v2t] if two_sets else [])
    in_specs = [pl.BlockSpec((None, tq, qw), lambda bi, i: (bi, i, 0))] + [full(t) for t in args[1:]]
    return pl.pallas_call(
        functools.partial(_gqa_body, n_heads=qw // HEAD_DIM, group=group, two_sets=two_sets, sub=min(256, tq)),
        grid=(b, nq // tq), in_specs=in_specs,
        out_specs=pl.BlockSpec((None, tq, qw), lambda bi, i: (bi, i, 0)),
        out_shape=jax.ShapeDtypeStruct((b, nq, qw), BF16),
        compiler_params=_params("parallel", "parallel"), name="gqa2" if two_sets else "gqa1",
    )(*args)


def _mha_body(q_ref, k_ref, v_ref, o_ref, *, n_heads):
    for h in range(n_heads):
        hs = slice(HEAD_DIM * h, HEAD_DIM * (h + 1))
        s = _nt_dot(q_ref[:, hs], k_ref[:, hs])
        p = jnp.exp2(s - jnp.max(s, axis=-1, keepdims=True))
        o = jnp.dot(p.astype(BF16), v_ref[:, hs], preferred_element_type=F32)
        o_ref[:, hs] = (o / jnp.sum(p, axis=-1, keepdims=True)).astype(o_ref.dtype)


def _mha(q, k, v, *, tq):
    b, nq, qw = q.shape
    full = lambda t: pl.BlockSpec((None,) + t.shape[1:], lambda bi, i: (bi, 0, 0))
    return pl.pallas_call(
        functools.partial(_mha_body, n_heads=qw // HEAD_DIM),
        grid=(b, nq // tq),
        in_specs=[pl.BlockSpec((None, tq, qw), lambda bi, i: (bi, i, 0)), full(k), full(v)],
        out_specs=pl.BlockSpec((None, tq, qw), lambda bi, i: (bi, i, 0)),
        out_shape=jax.ShapeDtypeStruct((b, nq, qw), BF16),
        compiler_params=_params("parallel", "parallel"), name="mha",
    )(q, k, v)


def _na_body(q_ref, k_ref, v_ref, kc_ref, vc_ref, bias_ref, o_ref, *, rt, k_r, rows):
    i = pl.program_id(1)
    kc = kc_ref[...]
    vc = vc_ref[...]
    lane_head = lax.broadcasted_iota(jnp.int32, (GRID_W, GW), 1) // HEAD_DIM
    def scores(j):
        r = i * rt + j
        rs = jnp.clip(r - k_r // 2, 0, rows - k_r)
        start = pl.multiple_of(rs * GRID_W, GRID_W)
        q = q_ref[j * GRID_W:(j + 1) * GRID_W, :]
        q4 = jnp.concatenate([jnp.where(lane_head == h, q, jnp.zeros_like(q)) for h in range(NA_HEADS)], axis=0)
        s_loc = (_nt_dot(q4, k_ref[pl.ds(start, k_r * GRID_W), :]) + bias_ref[r - rs]).astype(BF16)
        s_ctx = _nt_dot(q4, kc).astype(BF16)
        return start, s_loc, s_ctx

    def weighted_values(j, start, s_loc, s_ctx):
        m = jnp.maximum(jnp.max(s_loc, axis=-1, keepdims=True), jnp.max(s_ctx, axis=-1, keepdims=True))
        p_loc = jnp.exp2(s_loc - m)
        p_ctx = jnp.exp2(s_ctx - m)
        l = (jnp.sum(p_loc.astype(F32), axis=-1, keepdims=True)
             + jnp.sum(p_ctx.astype(F32), axis=-1, keepdims=True))
        o4 = (jnp.dot(p_loc, v_ref[pl.ds(start, k_r * GRID_W), :], preferred_element_type=F32)
              + jnp.dot(p_ctx, vc, preferred_element_type=F32)) / l
        o = jnp.zeros((GRID_W, GW), F32)
        for h in range(NA_HEADS):
            o = o + jnp.where(lane_head == h, o4[h * GRID_W:(h + 1) * GRID_W, :], 0.0)
        o_ref[j * GRID_W:(j + 1) * GRID_W, :] = o.astype(o_ref.dtype)

    nxt = scores(0)
    for j in range(rt):
        cur, nxt = nxt, (scores(j + 1) if j + 1 < rt else None)
        weighted_values(j, *cur)


def _na(q, k, v, kc, vc, bias, layer, rt):
    b, n, _ = q.shape
    rows = n // GRID_W
    k_r = bias.shape[1]
    full = lambda t: pl.BlockSpec((None,) + t.shape[1:], lambda bi, i: (bi, 0, 0))
    return pl.pallas_call(
        functools.partial(_na_body, rt=rt, k_r=k_r, rows=rows),
        grid=(b, rows // rt),
        in_specs=[pl.BlockSpec((None, rt * GRID_W, GW), lambda bi, i: (bi, i, 0)),
                  full(k), full(v), full(kc), full(vc), _layer_spec(bias, layer)],
        out_specs=pl.BlockSpec((None, rt * GRID_W, GW), lambda bi, i: (bi, i, 0)),
        out_shape=jax.ShapeDtypeStruct((b, n, GW), BF16),
        compiler_params=_params("parallel", "parallel"), name="na",
    )(q, k, v, kc, vc, bias)


def _na_bias_table(rel_bias, rows):
    k_r = min(NA_ROWS, rows)
    cols = np.arange(GRID_W)
    col_start = np.clip(cols - NA_COLS // 2, 0, GRID_W - NA_COLS)
    inside = (cols[None, :] >= col_start[:, None]) & (cols[None, :] < col_start[:, None] + NA_COLS)
    rel_c = cols[None, :] - cols[:, None] + (NA_COLS - 1)
    rel_r = np.arange(k_r)[None, :] - np.arange(k_r)[:, None] + (NA_ROWS - 1)
    pick_r = (rel_r[:, :, None] == np.arange(2 * NA_ROWS - 1)).astype(np.float32)
    pick_c = ((rel_c[:, :, None] == np.arange(2 * NA_COLS - 1)) & inside[:, :, None]).astype(np.float32)
    t = jnp.einsum("cai,lhim->lcham", pick_r, rel_bias.astype(F32), precision=HIGHEST)
    t = jnp.einsum("lcham,jkm->lchjak", t, pick_c, precision=HIGHEST)
    t = t + np.where(inside, 0.0, MASK_VALUE).astype(np.float32)[None, None, None, :, None, :]
    return (t * LOG2E).reshape(rel_bias.shape[0], k_r, NA_HEADS * GRID_W, k_r * GRID_W)


def _s5_scan(bu_ref, hb_ref, st_ref, a_ref, *, tc, nb, n_state, col_w, reverse):
    for c0 in range(0, n_state, col_w):
        re = slice(c0, c0 + col_w)
        im = slice(n_state + c0, n_state + c0 + col_w)
        ar = jnp.broadcast_to(a_ref[:, re], (nb, col_w))
        ai = jnp.broadcast_to(a_ref[:, im], (nb, col_w))

        hr, hi = st_ref[:, re], st_ref[:, im]
        for t in range(tc):
            rows = slice((tc - 1 - t if reverse else t) * nb, (tc - t if reverse else t + 1) * nb)
            hr, hi = ar * hr - ai * hi + bu_ref[rows, re], ar * hi + ai * hr + bu_ref[rows, im]
            hb_ref[rows, re] = hr.astype(hb_ref.dtype)
            hb_ref[rows, im] = hi.astype(hb_ref.dtype)
        st_ref[:, re] = hr
        st_ref[:, im] = hi


def _s5_fwd_body(uc_ref, ul_ref, perm_ref, bd_ref, cd_ref, a_ref, ut_ref, yf_ref, hs0_ref, hs1_ref, hb_ref,
                 st_ref, *, n_c, scan):
    j = pl.program_id(0)

    @pl.when(j == 0)
    def _():
        st_ref[...] = jnp.zeros_like(st_ref)
        hs1_ref[...] = jnp.zeros_like(hs1_ref)

    def step(cur, prev):
        nb, tc, w = uc_ref.shape
        ts = perm_ref.shape[0] // nb
        u_bm = jnp.where(j < n_c, uc_ref[...], ul_ref[...])
        ut = jnp.concatenate(
            [jnp.dot(perm_ref[...], u_bm[:, s:s + ts, :].reshape(nb * ts, w), preferred_element_type=F32)
             for s in range(0, tc, ts)], axis=0).astype(ut_ref.dtype)
        ut_ref[...] = ut
        cur[...] = _dot2(ut, bd_ref[...])
        scan(prev, hb_ref, st_ref, a_ref, reverse=False)
        yf_ref[...] = _dot2(hb_ref[...], cd_ref[...])

    pl.when(j % 2 == 0)(lambda: step(hs0_ref, hs1_ref))
    pl.when(j % 2 == 1)(lambda: step(hs1_ref, hs0_ref))


def _s5_bwd_body(ut_ref, utp_ref, yf_ref, permt_ref, bd_ref, cd_ref, a_ref, dsk_ref, wglu_ref, oc_ref, ol_ref,
                 hs0_ref, hs1_ref, hb_ref, st_ref, *, n_c, scan):
    j = pl.program_id(0)

    @pl.when(j == 0)
    def _():
        st_ref[...] = jnp.zeros_like(st_ref)
        hs1_ref[...] = jnp.zeros_like(hs1_ref)

    def step(cur, prev):
        cur[...] = _dot2(ut_ref[...], bd_ref[...])
        scan(prev, hb_ref, st_ref, a_ref, reverse=True)
        t = dsk_ref[...] * utp_ref[...].astype(F32) + yf_ref[...] + _dot2(hb_ref[...], cd_ref[...])
        g = jax.nn.gelu(t)
        ob = (g * jax.nn.sigmoid(_dot2(g.astype(BF16), wglu_ref[...]))).astype(BF16)
        nb, tc, w = oc_ref.shape
        rows = permt_ref.shape[0]
        groups = [jnp.dot(permt_ref[...], ob[r0:r0 + rows], preferred_element_type=F32).astype(BF16)
                  .reshape(nb, rows // nb, w) for r0 in range(0, tc * nb, rows)]

        def write(o_ref):
            for s, grp in enumerate(groups):
                o_ref[:, s * grp.shape[1]:(s + 1) * grp.shape[1], :] = grp

        pl.when(j <= n_c)(lambda: write(oc_ref))
        pl.when(j > n_c)(lambda: write(ol_ref))

    pl.when(j % 2 == 0)(lambda: step(hs0_ref, hs1_ref))
    pl.when(j % 2 == 1)(lambda: step(hs1_ref, hs0_ref))


def _s5(usc, us, bd, cd, a_row, dsk, wglu, layer, tc):
    nb, lc, _ = usc.shape
    n = us.shape[1]
    n_c, n_l = lc // tc, n // tc
    n_all = n_c + n_l
    n_state2 = bd.shape[-1]
    rows = tc * nb
    ts = math.gcd(tc, BF16_SUBLANES)
    r = np.arange(ts * nb)
    perm = np.zeros((ts * nb, ts * nb), np.float32)
    perm[r, (r % nb) * ts + r // nb] = 1.0
    scan = functools.partial(_s5_scan, tc=tc, nb=nb, n_state=n_state2 // 2, col_w=512)
    scratch = [pltpu.VMEM((rows, n_state2), F32), pltpu.VMEM((nb, n_state2), F32)]
    seg = lambda idx: pl.BlockSpec((nb, tc, GW), lambda j: (0, idx(j), 0))
    tm_rows = lambda idx: pl.BlockSpec((rows, GW), lambda j: (idx(j), 0))

    ut, yf = pl.pallas_call(
        functools.partial(_s5_fwd_body, n_c=n_c, scan=scan), grid=(n_all + 1,),
        in_specs=[seg(lambda j: jnp.minimum(j, n_c - 1)), seg(lambda j: jnp.clip(j - n_c, 0, n_l - 1)),
                  _const_spec(perm.shape)] + [_layer_spec(t, layer, 0) for t in (bd, cd, a_row)],
        out_specs=[tm_rows(lambda j: jnp.minimum(j, n_all - 1)), tm_rows(lambda j: jnp.maximum(j - 1, 0))],
        out_shape=[jax.ShapeDtypeStruct((n_all * rows, GW), BF16), jax.ShapeDtypeStruct((n_all * rows, GW), F32)],
        scratch_shapes=[scratch[0], scratch[0], pltpu.VMEM((rows, n_state2), BF16), scratch[1]],
        compiler_params=_params("arbitrary"), name="s5_fwd",
    )(usc, us, jnp.asarray(perm.astype(BF16)), bd, cd, a_row)

    def blk(j):
        j = jnp.clip(j, 0, n_all - 1)
        return jnp.where(j < n_c, n_c - 1 - j, n_c + n_all - 1 - j)

    prev = lambda j: jnp.maximum(j - 1, 0)
    return pl.pallas_call(
        functools.partial(_s5_bwd_body, n_c=n_c, scan=scan), grid=(n_all + 1,),
        in_specs=[tm_rows(blk), tm_rows(lambda j: blk(prev(j))), tm_rows(lambda j: blk(prev(j))),
                  _const_spec(perm.shape)] + [_layer_spec(t, layer, 1) for t in (bd, cd, a_row)]
        + [_layer_spec(dsk, layer), _layer_spec(wglu, layer)],
        out_specs=[seg(lambda j: jnp.maximum(n_c - 1 - prev(j), 0)),
                   seg(lambda j: n_l - 1 - jnp.maximum(prev(j) - n_c, 0))],
        out_shape=[jax.ShapeDtypeStruct(usc.shape, BF16), jax.ShapeDtypeStruct(us.shape, BF16)],
        scratch_shapes=[scratch[0], scratch[0], pltpu.VMEM((rows, n_state2), BF16), scratch[1]], compiler_params=_params("arbitrary"), name="s5_bwd",
    )(ut, ut, yf, jnp.asarray(perm.T.astype(BF16)), bd, cd, a_row, dsk, wglu)


def _s5_matrices(abr, abi, bbr, bbi, c_re, c_im):
    g, p, h = S5_GROUPS, S5_STATE, S5_GROUP_CH
    lead = abr.shape[:2]
    eye = np.eye(g, dtype=np.float32)[None, None, :, None, :, None]
    spread = lambda t: (jnp.swapaxes(t, -1, -2)[:, :, :, :, None, :] * eye).reshape(
        *lead, g * t.shape[-1], g * t.shape[-2])
    bd = jnp.concatenate([spread(bbr), spread(bbi)], axis=-1)
    cd = jnp.concatenate([spread(c_re.astype(F32)), spread(-c_im.astype(F32))], axis=-2)
    a_row = jnp.concatenate([abr.reshape(*lead, 1, g * p), abi.reshape(*lead, 1, g * p)], axis=-1)
    return bd.astype(BF16), cd.astype(BF16), a_row


def _fnet_body(u_ref, mirror_ref, dft_ref, ccs_ref, w_ref, b_ref, o_ref, ue_ref):
    h = u_ref.shape[0] // 2

    @pl.when(pl.program_id(1) == 0)
    def _():
        uf = _dot2(mirror_ref[...], u_ref[h:2 * h, :]).astype(BF16)
        t = jnp.dot(u_ref[0:h, :], ccs_ref[...], preferred_element_type=F32)
        tf = jnp.dot(uf, ccs_ref[...], preferred_element_type=F32)
        row0 = lax.broadcasted_iota(jnp.int32, (h, GW), 0) == 0
        ue_ref[0:h, :] = (t[:, :GW] + jnp.where(row0, 0.0, tf[:, :GW])).astype(BF16)
        ue_ref[h:2 * h, :] = jnp.where(row0, tf[:, :GW], t[:, GW:] - tf[:, GW:]).astype(BF16)

    tk = o_ref.shape[0]
    rows = pl.ds(pl.multiple_of(pl.program_id(1) * tk, tk), tk)
    f = _dot2(dft_ref[rows, :], ue_ref[...])
    o = jnp.dot(f.astype(BF16), w_ref[...], preferred_element_type=F32) + b_ref[...]
    o_ref[...] = o.astype(o_ref.dtype)


def _fnet(u, dft, ccs, w, bias, layer, tk):
    b, n, _ = u.shape
    h = n // 2
    mirror = np.zeros((h, h), np.float32)
    mirror[np.arange(h), (h - np.arange(h)) % h] = 1.0
    return pl.pallas_call(
        _fnet_body, grid=(b, n // tk),
        in_specs=[pl.BlockSpec((None, n, GW), lambda bi, k: (bi, 0, 0)), _const_spec(mirror.shape),
                  _const_spec(dft.shape),
                  _const_spec(ccs.shape), _layer_spec(w, layer), _layer_spec(bias, layer)],
        out_specs=pl.BlockSpec((None, tk, GW), lambda bi, k: (bi, k, 0)),
        out_shape=jax.ShapeDtypeStruct((b, n, GW), BF16),
        scratch_shapes=[pltpu.VMEM((n, GW), BF16)],
        compiler_params=_params("parallel", "arbitrary"), name="fnet",
    )(u, jnp.asarray(mirror.astype(BF16)), dft, ccs, w, bias)


def _dft_matrices(n):
    ch = GW // FN_GROUPS
    h = n // 2
    k = np.arange(n)
    ang = (2.0 * np.pi / n) * ((k[:, None] * np.arange(h)[None, :]) % n)
    m = -np.sin(ang)
    m[:, 0] = 1.0 - 2.0 * (k % 2)
    dft = np.concatenate([np.cos(ang), m], axis=1)
    c = np.arange(GW)
    same = (c[:, None] // ch) == (c[None, :] // ch)
    angc = (2.0 * np.pi / ch) * (((c[:, None] % ch) * (c[None, :] % ch)) % ch)
    norm = 1.0 / math.sqrt(n * ch)
    ccs = np.concatenate([np.where(same, np.cos(angc), 0.0), np.where(same, np.sin(angc), 0.0)], axis=1) * norm
    return (jnp.asarray(dft.astype(np.float32)).astype(BF16),
            jnp.asarray(ccs.astype(np.float32)).astype(BF16))


def _out_mlp_body(x_ref, oa_ref, ob_ref, on_ref, od_ref, g1_ref, sh2_ref, sc2_ref, g2_ref,
                  wout_ref, gpm_ref, gpre_ref, gpost_ref, w1_ref, w2_ref, o_ref, *, ff_chunk, sub):
    d_ff = w1_ref.shape[1]
    for r0 in range(0, x_ref.shape[0], sub):
        rs = slice(r0, r0 + sub)
        cat = jnp.concatenate([oa_ref[rs, :], ob_ref[rs, :], on_ref[rs, :], od_ref[rs, :]], axis=-1)
        y = jnp.dot(cat, wout_ref[...], preferred_element_type=F32)
        x1 = x_ref[rs, :] + g1_ref[...] * _rms(y, gpm_ref[...])
        h = (_rms(x1, gpre_ref[...]) * (1.0 + sc2_ref[...]) + sh2_ref[...]).astype(BF16)
        m = jnp.zeros(x1.shape, F32)
        for c0 in range(0, d_ff, ff_chunk):
            a = jnp.maximum(jnp.dot(h, w1_ref[:, c0:c0 + ff_chunk], preferred_element_type=F32), 0.0)
            m = m + jnp.dot((a * a).astype(BF16), w2_ref[c0:c0 + ff_chunk, :], preferred_element_type=F32)
        o_ref[rs, :] = x1 + g2_ref[...] * _rms(m, gpost_ref[...])


def _out_mlp(x, oa, ob, on, od, mod_t, layer, mod_row, wout, gpm, gpre, gpost, w1, w2, tm):
    b, n, d = x.shape
    row = lambda w: pl.BlockSpec((None, tm, w), lambda bi, i: (bi, i, 0))
    consts = [wout, gpm, gpre, gpost, w1, w2]
    return pl.pallas_call(
        functools.partial(_out_mlp_body, ff_chunk=1024, sub=min(512, tm)),
        grid=(b, n // tm),
        in_specs=[row(d), row(GW), row(GW), row(GW), row(GW)]
        + [_mod_spec(mod_t, layer, which, mod_row) for which in (2, 3, 4, 5)]
        + [_layer_spec(t, layer) for t in consts],
        out_specs=row(d), out_shape=jax.ShapeDtypeStruct((b, n, d), F32),
        compiler_params=_params("parallel", "parallel"), name="out_mlp",
    )(x, oa, ob, on, od, mod_t, mod_t, mod_t, mod_t, *consts)


def _rope_tables(n):
    half = HEAD_DIM // 4
    pos = np.arange(n)
    jj = np.arange(GW) % HEAD_DIM
    inv = ROPE_BASE ** (-(jj % half) / half)
    p = np.where((jj // (2 * half) == 0)[None, :], (pos // GRID_W)[:, None], (pos % GRID_W)[:, None])
    ang = p * inv[None, :]
    first = ((jj % (2 * half)) < half)[None, :]
    sin = np.sin(ang)
    tabs = (np.cos(ang), np.where(first, -sin, 0.0), np.where(first, 0.0, sin))
    return tuple(jnp.asarray(t.astype(np.float32)) for t in tabs)


def kernel(x, c, ctx, c_ctx, w_ada, b_ada, g_pre_mix, g_post_mix, g_pre_mlp, g_post_mlp, w_in, g_q_attn, g_k_attn,
           s5_a_re, s5_a_im, s5_log_dt, s5_b_re, s5_b_im, s5_c_re, s5_c_im, s5_d, w_s5_glu, na_rel_bias,
           w_fnet, b_fnet, w_out, w_mlp1, w_mlp2):
    b, n, d = x.shape
    lc = ctx.shape[1]
    depth = w_ada.shape[0]
    rows = n // GRID_W

    n_rows = -(-(b + 1) // 8) * 8
    cc = jnp.concatenate([c, c_ctx[None, :], jnp.zeros((n_rows - b - 1, d), c.dtype)], axis=0)
    mod_t = _ada(cc, w_ada, b_ada)

    vec = lambda t: t.reshape(depth, 1, -1).astype(F32)
    abr, abi, bbr, bbi = _s5_prep(s5_a_re, s5_a_im, s5_log_dt, s5_b_re, s5_b_im)
    bd, cd, a_row = _s5_matrices(abr, abi, bbr, bbi, s5_c_re, s5_c_im)
    na_bias = _na_bias_table(na_rel_bias, rows)
    w_in_b, w_glu_b, w_fn_b = w_in.astype(BF16), w_s5_glu.astype(BF16), w_fnet.astype(BF16)
    mlp_consts = (w_out.astype(BF16), vec(g_post_mix), vec(g_pre_mlp), vec(g_post_mlp),
                  w_mlp1.astype(BF16), w_mlp2.astype(BF16))
    g_pre, d_skip, b_fn = vec(g_pre_mix), vec(s5_d), vec(b_fnet)
    gq = jnp.tile(vec(g_q_attn), (1, 1, ATT_HEADS))
    gk = jnp.tile(vec(g_k_attn), (1, 1, ATT_KV_HEADS))
    rope_tabs = _rope_tables(n)
    lane = np.arange(GW)
    hm = jnp.asarray(np.where((lane[:, None] // HEAD_DIM) == (lane[None, :] // HEAD_DIM), 1.0 / HEAD_DIM, 0.0)
                     .astype(np.float32).astype(BF16))
    dft_n, ccs_n = _dft_matrices(n)
    dft_c, ccs_c = _dft_matrices(lc)

    tm_lat = min(1024, n)
    tm_c = min(256, lc)
    tc = math.gcd(64, math.gcd(n, lc))
    group = ATT_HEADS // ATT_KV_HEADS

    xc = ctx
    for l in range(depth):
        need_ctx = l < depth - 1
        qa, ka, va, us, nq, nk, nv, fn = _in_proj(x, mod_t, l, None, g_pre, w_in_b, gq, gk, hm, rope_tabs, tm_lat)
        qac, kac, vac, usc, nqc, nkc, nvc, fnc = _in_proj(xc, mod_t, l, b, g_pre, w_in_b, gq, gk, hm, None, tm_c)

        oa = _gqa(qa, ka, va, kac, vac, group=group, tq=min(1024, n))
        on = _na(nq, nk, nv, nkc, nvc, na_bias, l, rt=min(32, rows))
        od = _fnet(fn, dft_n, ccs_n, w_fn_b, b_fn, l, tk=min(512, n))
        obc, ob = _s5(usc, us, bd, cd, a_row, d_skip, w_glu_b, l, tc)
        x_new = _out_mlp(x, oa, ob, on, od, mod_t, l, None, *mlp_consts, tm_lat)
        if need_ctx:
            oac = _gqa(qac, kac, vac, group=group, tq=tm_c)
            onc = _mha(nqc, nkc, nvc, tq=tm_c)
            odc = _fnet(fnc, dft_c, ccs_c, w_fn_b, b_fn, l, tk=tm_c)
            xc = _out_mlp(xc, oac, obc, onc, odc, mod_t, l, b, *mlp_consts, tm_c)
        x = x_new
    return x
```
